```python
import jax, jax.numpy as jnp
from jax import lax
import numpy as np

D_MODEL = 1024
BATCH = 2
SEQ = 8192
DEPTH = 1
DEC_BATCH = 8
DEC_SEQ = 32
PAST_LEN = 1024

CHUNK = 64
D_CONV = D_MODEL // 2
D_LRU = D_MODEL - D_CONV
N_LRU_HEADS = 8
LRU_HEAD_DIM = D_LRU // N_LRU_HEADS
CONV_A_WIDTH = 3
CONV_B_WIDTH = 4
LRU_C = 8.0
N_GROUPS = 4
EXPERTS_PER_GROUP = 4
N_EXPERTS = N_GROUPS * EXPERTS_PER_GROUP
TOP_K_INNER = 2
D_EXPERT = D_MODEL // 4
D_IN_PROJ = 3 * D_CONV + 2 * D_LRU
ALPHA = (2.0 * DEPTH) ** 0.25
BETA = (8.0 * DEPTH) ** -0.25
LN_EPS = 1e-5

kernel_name = "hybrid_conv_rglru_hiermoe_stream_step"


def layer_norm(x, g, b):
    xf = x.astype(jnp.float32)
    mu = jnp.mean(xf, axis=-1, keepdims=True)
    var = jnp.mean(jnp.square(xf - mu), axis=-1, keepdims=True)
    return ((xf - mu) * lax.rsqrt(var + LN_EPS) * g.astype(jnp.float32)
            + b.astype(jnp.float32)).astype(x.dtype)


def causal_depthwise_conv(u, prev, w):
    width = w.shape[0]
    T = u.shape[1]
    up = jnp.concatenate([prev.astype(u.dtype), u], axis=1)
    y = up[:, 0:T] * w[0]
    for k in range(1, width):
        y = y + up[:, k:k + T] * w[k]
    return y, up[:, up.shape[1] - (width - 1):]


def rg_lru(xc, h_prev, wa, ba, wx, bx, lam):
    bsz, T, _ = xc.shape
    xh = xc.reshape(bsz, T, N_LRU_HEADS, LRU_HEAD_DIM)
    r = jax.nn.sigmoid(jnp.einsum('bthi,hij->bthj', xh, wa).reshape(bsz, T, D_LRU) + ba)
    i = jax.nn.sigmoid(jnp.einsum('bthi,hij->bthj', xh, wx).reshape(bsz, T, D_LRU) + bx)
    log_a = -LRU_C * r.astype(jnp.float32) * jax.nn.softplus(-lam.astype(jnp.float32))
    a = jnp.exp(log_a)
    b = jnp.sqrt(-jnp.expm1(2.0 * log_a)) * (i * xc).astype(jnp.float32)
    b = b.at[:, 0].add(a[:, 0] * h_prev.astype(jnp.float32))

    def combine(left, right):
        a1, b1 = left
        a2, b2 = right
        return a1 * a2, a2 * b1 + b2

    _, h = lax.associative_scan(combine, (a, b), axis=1)
    return h.astype(xc.dtype), h[:, -1].astype(h_prev.dtype)


def hybrid_mixer(x, conv_a_prev, conv_b_prev, h_prev, w_in, conv_a_w, conv_b_w,
                 conv_b_bias, lru_wa, lru_ba, lru_wx, lru_bx, lru_lambda, w_out):
    proj = x @ w_in
    gate_b, gate_c, val_a, val_b, gate_lru = jnp.split(
        proj, [D_CONV, 2 * D_CONV, 3 * D_CONV, 3 * D_CONV + D_LRU], axis=-1)
    u = gate_c * val_a
    uc, new_conv_a = causal_depthwise_conv(u, conv_a_prev, conv_a_w)
    y_a = gate_b * uc
    xc, new_conv_b = causal_depthwise_conv(val_b, conv_b_prev, conv_b_w)
    xc = xc + conv_b_bias
    h, new_h = rg_lru(xc, h_prev, lru_wa, lru_ba, lru_wx, lru_bx, lru_lambda)
    y_b = h * jax.nn.gelu(gate_lru)
    out = jnp.concatenate([y_a, y_b], axis=-1) @ w_out
    return out, new_conv_a.astype(conv_a_prev.dtype), new_conv_b.astype(conv_b_prev.dtype), new_h


def hier_moe(x, w_router_group, w_router_expert, moe_w1, moe_w3, moe_w2):
    shp = x.shape
    xt = x.reshape(-1, D_MODEL)
    p_grp = jax.nn.softmax((xt @ w_router_group).astype(jnp.float32), axis=-1)
    g_idx = jnp.argmax(p_grp, axis=-1)
    g_gate = jnp.take_along_axis(p_grp, g_idx[:, None], axis=-1)
    e_logits = (xt @ w_router_expert).astype(jnp.float32).reshape(-1, N_GROUPS, EXPERTS_PER_GROUP)
    e_logits = jnp.take_along_axis(e_logits, g_idx[:, None, None], axis=1)[:, 0]
    top_v, top_i = lax.top_k(e_logits, TOP_K_INNER)
    top_w = jax.nn.softmax(top_v, axis=-1) * g_gate
    expert_id = g_idx[:, None] * EXPERTS_PER_GROUP + top_i
    gates = jnp.sum(jax.nn.one_hot(expert_id, N_EXPERTS, dtype=jnp.float32)
                    * top_w[..., None], axis=1)
    h = jax.nn.silu(jnp.einsum('nd,edf->nef', xt, moe_w1)) * jnp.einsum('nd,edf->nef', xt, moe_w3)
    h = h * gates.astype(h.dtype)[..., None]
    y = jnp.einsum('nef,efd->nd', h, moe_w2)
    return y.reshape(shp)


def trunk_layer(x, conv_a_prev, conv_b_prev, h_prev, ln1_g, ln1_b, w_in, conv_a_w,
                conv_b_w, conv_b_bias, lru_wa, lru_ba, lru_wx, lru_bx, lru_lambda, w_out,
                ln2_g, ln2_b, w_router_group, w_router_expert, moe_w1, moe_w3, moe_w2):
    m, nca, ncb, nh = hybrid_mixer(x, conv_a_prev, conv_b_prev, h_prev, w_in, conv_a_w,
                                   conv_b_w, conv_b_bias, lru_wa, lru_ba, lru_wx, lru_bx,
                                   lru_lambda, w_out)
    x = layer_norm(ALPHA * x + m, ln1_g, ln1_b)
    x = layer_norm(ALPHA * x + hier_moe(x, w_router_group, w_router_expert, moe_w1, moe_w3, moe_w2),
                   ln2_g, ln2_b)
    return x, nca, ncb, nh


def setup_inputs(seed: int = 0) -> dict:
    key = jax.random.key(seed)
    ks = jax.random.split(key, 24)
    f32 = jnp.float32
    nrm = lambda k, shp, s: jax.random.normal(k, shp, f32) * s
    L = DEPTH
    u = jax.random.uniform(ks[13], (L, D_LRU), f32, 0.9, 0.999)
    a_base = u ** (1.0 / LRU_C)
    return {
        "x_prompt": nrm(ks[0], (BATCH, SEQ, D_MODEL), 1.0),
        "x_sample": nrm(ks[1], (DEC_BATCH, DEC_SEQ, D_MODEL), 1.0),
        "state_conv_a": nrm(ks[2], (L, DEC_BATCH, CONV_A_WIDTH - 1, D_CONV), 1.0),
        "state_conv_b": nrm(ks[3], (L, DEC_BATCH, CONV_B_WIDTH - 1, D_LRU), 1.0),
        "state_lru_h": nrm(ks[4], (L, DEC_BATCH, D_LRU), 0.5),
        "ln1_g": 1.0 + nrm(ks[5], (L, D_MODEL), 0.02),
        "ln1_b": nrm(ks[6], (L, D_MODEL), 0.02),
        "w_in": nrm(ks[7], (L, D_MODEL, D_IN_PROJ), D_MODEL ** -0.5),
        "conv_a_w": nrm(ks[8], (L, CONV_A_WIDTH, D_CONV), CONV_A_WIDTH ** -0.5),
        "conv_b_w": nrm(ks[9], (L, CONV_B_WIDTH, D_LRU), CONV_B_WIDTH ** -0.5),
        "conv_b_bias": nrm(ks[10], (L, D_LRU), 0.02),
        "lru_wa": nrm(ks[11], (L, N_LRU_HEADS, LRU_HEAD_DIM, LRU_HEAD_DIM), LRU_HEAD_DIM ** -0.5),
        "lru_ba": nrm(ks[12], (L, D_LRU), 0.1),
        "lru_wx": nrm(ks[14], (L, N_LRU_HEADS, LRU_HEAD_DIM, LRU_HEAD_DIM), LRU_HEAD_DIM ** -0.5),
        "lru_bx": nrm(ks[15], (L, D_LRU), 0.1),
        "lru_lambda": jnp.log(a_base) - jnp.log1p(-a_base),
        "w_out": nrm(ks[16], (L, D_MODEL, D_MODEL), BETA * D_MODEL ** -0.5),
        "ln2_g": 1.0 + nrm(ks[17], (L, D_MODEL), 0.02),
        "ln2_b": nrm(ks[18], (L, D_MODEL), 0.02),
        "w_router_group": nrm(ks[19], (L, D_MODEL, N_GROUPS), D_MODEL ** -0.5),
        "w_router_expert": nrm(ks[20], (L, D_MODEL, N_EXPERTS), D_MODEL ** -0.5),
        "moe_w1": nrm(ks[21], (L, N_EXPERTS, D_MODEL, D_EXPERT), D_MODEL ** -0.5),
        "moe_w3": nrm(ks[22], (L, N_EXPERTS, D_MODEL, D_EXPERT), D_MODEL ** -0.5),
        "moe_w2": nrm(ks[23], (L, N_EXPERTS, D_EXPERT, D_MODEL), BETA * D_EXPERT ** -0.5),
    }


def reference(x_prompt, x_sample, state_conv_a, state_conv_b, state_lru_h, ln1_g, ln1_b,
              w_in, conv_a_w, conv_b_w, conv_b_bias, lru_wa, lru_ba, lru_wx, lru_bx,
              lru_lambda, w_out, ln2_g, ln2_b, w_router_group, w_router_expert,
              moe_w1, moe_w3, moe_w2):
    bp = x_prompt.shape[0]
    yp, ys = x_prompt, x_sample
    ca_p, cb_p, h_p, ca_s, cb_s, h_s = [], [], [], [], [], []
    for l in range(DEPTH):
        w = (ln1_g[l], ln1_b[l], w_in[l], conv_a_w[l], conv_b_w[l], conv_b_bias[l],
             lru_wa[l], lru_ba[l], lru_wx[l], lru_bx[l], lru_lambda[l], w_out[l],
             ln2_g[l], ln2_b[l], w_router_group[l], w_router_expert[l],
             moe_w1[l], moe_w3[l], moe_w2[l])
        yp, a1, b1, h1 = trunk_layer(
            yp,
            jnp.zeros((bp, CONV_A_WIDTH - 1, D_CONV), state_conv_a.dtype),
            jnp.zeros((bp, CONV_B_WIDTH - 1, D_LRU), state_conv_b.dtype),
            jnp.zeros((bp, D_LRU), state_lru_h.dtype), *w)
        ys, a2, b2, h2 = trunk_layer(ys, state_conv_a[l], state_conv_b[l], state_lru_h[l], *w)
        ca_p.append(a1); cb_p.append(b1); h_p.append(h1)
        ca_s.append(a2); cb_s.append(b2); h_s.append(h2)
    return (yp, ys, jnp.stack(ca_p), jnp.stack(cb_p), jnp.stack(h_p),
            jnp.stack(ca_s), jnp.stack(cb_s), jnp.stack(h_s))
```

```python
import functools

import jax
import jax.numpy as jnp
from jax import lax
from jax.experimental import pallas as pl
from jax.experimental.pallas import tpu as pltpu

F32 = jnp.float32
BF16 = jnp.bfloat16

D_MODEL = 1024
D_HALF = 512
N_HEADS = 8
HEAD_DIM = 64
LRU_C = 8.0
N_GROUPS = 4
GROUP_SIZE = 4
N_EXPERTS = 16
D_EXPERT = 256
N_BUCKETS = 24
ALPHA = 2.0 ** 0.25
LN_EPS = 1e-5

SUBLANES = 8
LANES = 128
ROW_W = D_MODEL + LANES
VMEM_LIMIT = 56 * 2 ** 20

PROMPT_CHUNK = 64
MOE_TILE = 256
PERM_BLOCK = 512


def _layer_norm(z, g, b):
    mu = jnp.mean(z, axis=-1, keepdims=True)
    zc = z - mu
    var = jnp.mean(zc * zc, axis=-1, keepdims=True)
    return zc * lax.rsqrt(var + LN_EPS) * g + b


def _sigmoid(x):
    return 1.0 / (1.0 + jnp.exp(-x))


def _gelu_tanh(x):
    return 0.5 * x * (1.0 + jnp.tanh(0.7978845608028654 * (x + 0.044715 * (x * x * x))))


def _bdot(a, b):
    return jnp.dot(a, b, preferred_element_type=F32)


def _mixer_kernel(x_hbm, ua0_ref, vb0_ref, h0_ref, win_ref, wg_ref, wout_ref, wr_ref,
                  caw_ref, cbw_ref, vec_ref, ln_ref,
                  x1r_ref, cnt_ref, uat_ref, vbt_ref, ht_ref,
                  xbuf, sem, ubuf, vbuf, a_s, b_s, hl_s, ca_s, uac, vbc, hc, tri,
                  *, C, nT, total, chained):
    R = SUBLANES * C
    t = pl.program_id(1)
    g = pl.program_id(0) * nT + t
    slot = g % 2

    def x_copy(blk, sl, s):
        return pltpu.make_async_copy(x_hbm.at[blk * SUBLANES + s], xbuf.at[sl, :, s, :], sem.at[sl])

    @pl.when(g == 0)
    def _():
        for s in range(SUBLANES):
            x_copy(0, 0, s).start()
        rr = lax.broadcasted_iota(jnp.int32, (R, R), 0)
        cc = lax.broadcasted_iota(jnp.int32, (R, R), 1)
        tri[...] = jnp.where(cc < rr, 1.0, 0.0).astype(BF16)

    for s in range(SUBLANES):
        x_copy(g, slot, s).wait()

    @pl.when(g + 1 < total)
    def _():
        for s in range(SUBLANES):
            x_copy(g + 1, 1 - slot, s).start()

    @pl.when(t == 0)
    def _():
        uac[...] = ua0_ref[0]
        vbc[...] = vb0_ref[0]
        hc[...] = h0_ref[0]

    x = xbuf[slot].reshape(R, D_MODEL)
    xb = x.astype(BF16)

    def proj(k):
        return _bdot(xb, win_ref[:, k * D_HALF:(k + 1) * D_HALF])

    gate_b = proj(0)
    u = proj(1) * proj(2)
    ubuf[2 * SUBLANES:2 * SUBLANES + R, :] = u
    vbuf[3 * SUBLANES:3 * SUBLANES + R, :] = proj(3)
    gate_l = proj(4)

    sub = lax.broadcasted_iota(jnp.int32, (SUBLANES, D_HALF), 0)

    def fill_history(buf, carry, tail_ref, width):
        for k in range(width):
            rows = slice(k * SUBLANES, (k + 1) * SUBLANES)
            tail = buf[R + k * SUBLANES:R + (k + 1) * SUBLANES, :]
            tail_ref[0, rows, :] = tail
            if chained:
                prev_chunk = pltpu.roll(tail, 1, 0)
                buf[rows, :] = jnp.where(sub == 0, carry[rows, :], prev_chunk)
                carry[rows, :] = prev_chunk
            else:
                buf[rows, :] = carry[rows, :]

    fill_history(ubuf, uac, uat_ref, 2)
    fill_history(vbuf, vbc, vbt_ref, 3)

    caw = caw_ref[...]
    uc = (caw[0:1] * ubuf[0:R, :] + caw[1:2] * ubuf[SUBLANES:SUBLANES + R, :]
          + caw[2:3] * ubuf[2 * SUBLANES:2 * SUBLANES + R, :])
    y_a = gate_b * uc

    cbw = cbw_ref[...]
    vec = vec_ref[...]
    xc = (cbw[0:1] * vbuf[0:R, :] + cbw[1:2] * vbuf[SUBLANES:SUBLANES + R, :]
          + cbw[2:3] * vbuf[2 * SUBLANES:2 * SUBLANES + R, :]
          + cbw[3:4] * vbuf[3 * SUBLANES:3 * SUBLANES + R, :]) + vec[0:1]
    xcb = xc.astype(BF16)

    neg_lam = -vec[3:4]
    softplus = jnp.maximum(neg_lam, 0.0) + jnp.log1p(jnp.exp(-jnp.abs(neg_lam)))
    half = D_HALF // 2
    for hf in range(2):
        cs = slice(hf * half, (hf + 1) * half)
        gg = _bdot(xcb[:, cs], wg_ref[hf])
        r = _sigmoid(gg[:, :half] + vec[1:2, cs])
        ig = _sigmoid(gg[:, half:] + vec[2:3, cs])
        log_a = (-LRU_C) * r * softplus[:, cs]
        a = jnp.exp(log_a)
        mult = jnp.sqrt(-jnp.tanh(log_a) * (1.0 + a * a))
        a_s[:, cs] = a
        b_s[:, cs] = mult * (ig * xc[:, cs])

    def scan_step(j, carry):
        hl, ca = carry
        rows = pl.ds(pl.multiple_of(j * SUBLANES, SUBLANES), SUBLANES)
        a = a_s[rows, :]
        hl = a * hl + b_s[rows, :]
        ca = a * ca
        hl_s[rows, :] = hl
        ca_s[rows, :] = ca
        return hl, ca

    zeros = jnp.zeros((SUBLANES, D_HALF), F32)
    b_tot, a_tot = lax.fori_loop(0, C, scan_step, (zeros, zeros + 1.0), unroll=8)

    h_carry = hc[...]
    if chained:
        a_inc, b_inc = a_tot, b_tot
        for d in (1, 2, 4):
            a_sh = pltpu.roll(a_inc, d, 0)
            b_sh = pltpu.roll(b_inc, d, 0)
            keep = sub >= d
            b_inc = jnp.where(keep, a_inc * b_sh + b_inc, b_inc)
            a_inc = jnp.where(keep, a_inc * a_sh, a_inc)
        h_end = b_inc + a_inc * h_carry
        h_in = jnp.where(sub == 0, h_carry, pltpu.roll(h_end, 1, 0))
        hc[...] = jnp.broadcast_to(h_end[SUBLANES - 1:SUBLANES, :], (SUBLANES, D_HALF))
    else:
        h_in = h_carry
        h_end = b_tot + a_tot * h_in
    ht_ref[0] = h_end

    h = (hl_s[...].reshape(C, SUBLANES, D_HALF)
         + ca_s[...].reshape(C, SUBLANES, D_HALF) * h_in[None]).reshape(R, D_HALF)
    y_b = h * _gelu_tanh(gate_l)

    m = _bdot(y_a.astype(BF16), wout_ref[0:D_HALF, :]) + _bdot(y_b.astype(BF16), wout_ref[D_HALF:, :])
    lnp = ln_ref[...]
    x1 = _layer_norm(ALPHA * x + m, lnp[0:1], lnp[1:2])
    x1r_ref[:, 0:D_MODEL] = x1

    logits = _bdot(x1.astype(BF16), wr_ref[...])
    lane = lax.broadcasted_iota(jnp.int32, (R, LANES), 1).astype(F32)
    neg = jnp.float32(-jnp.inf)
    big = jnp.float32(1e9)

    def top1(vals):
        vmax = jnp.max(vals, axis=-1, keepdims=True)
        idx = jnp.min(jnp.where(vals == vmax, lane, big), axis=-1, keepdims=True)
        return vmax, idx

    grp_logits = jnp.where(lane < N_GROUPS, logits, neg)
    gmax, gidx = top1(grp_logits)
    g_gate = 1.0 / jnp.sum(jnp.exp(grp_logits - gmax), axis=-1, keepdims=True)
    e_base = N_GROUPS + GROUP_SIZE * gidx
    e_logits = jnp.where((lane >= e_base) & (lane < e_base + GROUP_SIZE), logits, neg)
    v1, i1 = top1(e_logits)
    v2, i2 = top1(jnp.where(lane == i1, neg, e_logits))
    e21 = jnp.exp(v2 - v1)
    w1 = g_gate / (1.0 + e21)
    w2 = g_gate * e21 / (1.0 + e21)
    l1 = i1 - e_base
    l2 = i2 - e_base
    first_low = l1 < l2
    lo = jnp.where(first_low, l1, l2)
    hi = jnp.where(first_low, l2, l1)
    w_lo = jnp.where(first_low, w1, w2)
    w_hi = jnp.where(first_low, w2, w1)
    pair = lo * (7.0 - lo) * 0.5 + (hi - lo - 1.0)
    bucket = gidx * 6.0 + pair

    onehot = lane == bucket
    before = _bdot(tri[...], jnp.where(onehot, 1.0, 0.0).astype(BF16))
    rank = jnp.sum(jnp.where(onehot, before, 0.0), axis=-1, keepdims=True)
    route = jnp.where(lane == 0, bucket,
                      jnp.where(lane == 1, w_lo,
                                jnp.where(lane == 2, w_hi,
                                          jnp.where(lane == 3, rank, 0.0))))
    x1r_ref[:, D_MODEL:ROW_W] = route
    cnt = jnp.sum(jnp.where(onehot, 1.0, 0.0), axis=0, keepdims=True)
    cnt_ref[0] = jnp.broadcast_to(cnt, (SUBLANES, LANES))


def _mixer_call(x3, ua0, vb0, h0, weights, *, C, nB, nT, chained):
    R = SUBLANES * C
    total = nB * nT
    win, wg, wout, wr, caw, cbw, vec, lnp = weights
    const = lambda shape: pl.BlockSpec(shape, lambda b, t: (0,) * len(shape))
    per_b = lambda rows: pl.BlockSpec((1, rows, D_HALF), lambda b, t: (b, 0, 0))
    kern = functools.partial(_mixer_kernel, C=C, nT=nT, total=total, chained=chained)
    return pl.pallas_call(
        kern,
        grid=(nB, nT),
        in_specs=[
            pl.BlockSpec(memory_space=pl.ANY),
            per_b(2 * SUBLANES), per_b(3 * SUBLANES), per_b(SUBLANES),
            const(win.shape), const(wg.shape), const(wout.shape), const(wr.shape),
            const(caw.shape), const(cbw.shape), const(vec.shape), const(lnp.shape),
        ],
        out_specs=[
            pl.BlockSpec((R, ROW_W), lambda b, t: (b * nT + t, 0)),
            pl.BlockSpec((1, SUBLANES, LANES), lambda b, t: (b * nT + t, 0, 0)),
            per_b(2 * SUBLANES), per_b(3 * SUBLANES), per_b(SUBLANES),
        ],
        out_shape=[
            jax.ShapeDtypeStruct((total * R, ROW_W), F32),
            jax.ShapeDtypeStruct((total, SUBLANES, LANES), F32),
            jax.ShapeDtypeStruct((nB, 2 * SUBLANES, D_HALF), F32),
            jax.ShapeDtypeStruct((nB, 3 * SUBLANES, D_HALF), F32),
            jax.ShapeDtypeStruct((nB, SUBLANES, D_HALF), F32),
        ],
        scratch_shapes=[
            pltpu.VMEM((2, C, SUBLANES, D_MODEL), F32),
            pltpu.SemaphoreType.DMA((2,)),
            pltpu.VMEM((R + 2 * SUBLANES, D_HALF), F32),
            pltpu.VMEM((R + 3 * SUBLANES, D_HALF), F32),
            pltpu.VMEM((R, D_HALF), F32),
            pltpu.VMEM((R, D_HALF), F32),
            pltpu.VMEM((R, D_HALF), F32),
            pltpu.VMEM((R, D_HALF), F32),
            pltpu.VMEM((2 * SUBLANES, D_HALF), F32),
            pltpu.VMEM((3 * SUBLANES, D_HALF), F32),
            pltpu.VMEM((SUBLANES, D_HALF), F32),
            pltpu.VMEM((R, R), BF16),
        ],
        compiler_params=pltpu.CompilerParams(
            dimension_semantics=("arbitrary", "arbitrary"), vmem_limit_bytes=VMEM_LIMIT),
        name="mixer_chained" if chained else "mixer_states",
    )(x3, ua0, vb0, h0, win, wg, wout, wr, caw, cbw, vec, lnp)


def _row_copy(src, dst, sem):
    return pltpu.make_async_copy(src, dst, sem)


def _scatter_kernel(pos_ref, xp_ref, xs_ref, out_hbm, sem, *, n_prompt_blocks, n_sample):
    g = pl.program_id(0)

    def run(src_ref, base, n):
        def start(r, c):
            _row_copy(src_ref.at[pl.ds(r, 1)], out_hbm.at[pl.ds(pos_ref[base + r], 1)], sem.at[0]).start()
            return c

        def wait(r, c):
            _row_copy(src_ref.at[pl.ds(r, 1)], out_hbm.at[pl.ds(pos_ref[base + r], 1)], sem.at[0]).wait()
            return c

        lax.fori_loop(0, n, start, 0, unroll=8)
        lax.fori_loop(0, n, wait, 0, unroll=8)

    @pl.when(g < n_prompt_blocks)
    def _():
        run(xp_ref, g * PERM_BLOCK, PERM_BLOCK)

    @pl.when(g == n_prompt_blocks)
    def _():
        run(xs_ref, n_prompt_blocks * PERM_BLOCK, n_sample)


def _gather_kernel(pos_ref, src_hbm, yp_ref, ys_ref, sem, *, n_prompt_blocks, n_sample):
    g = pl.program_id(0)

    def run(dst_ref, base, n):
        def start(r, c):
            _row_copy(src_hbm.at[pl.ds(pos_ref[base + r], 1)], dst_ref.at[pl.ds(r, 1)], sem.at[0]).start()
            return c

        def wait(r, c):
            _row_copy(src_hbm.at[pl.ds(pos_ref[base + r], 1)], dst_ref.at[pl.ds(r, 1)], sem.at[0]).wait()
            return c

        lax.fori_loop(0, n, start, 0, unroll=8)
        lax.fori_loop(0, n, wait, 0, unroll=8)

    @pl.when(g < n_prompt_blocks)
    def _():
        run(yp_ref, g * PERM_BLOCK, PERM_BLOCK)

    @pl.when(g == n_prompt_blocks)
    def _():
        run(ys_ref, n_prompt_blocks * PERM_BLOCK, n_sample)


def _moe_kernel(vt_ref, vlo_ref, vhi_ref, velo_ref, vehi_ref,
                x_ref, w1l_ref, w3l_ref, w2l_ref, w1h_ref, w3h_ref, w2h_ref, ln_ref, o_ref):
    v = pl.program_id(0)
    row_lo = vlo_ref[v]
    row_hi = vhi_ref[v]
    tile_start = vt_ref[v] * MOE_TILE

    @pl.when(row_hi > row_lo)
    def _():
        x = x_ref[:, 0:D_MODEL]
        route = x_ref[:, D_MODEL:ROW_W]
        xb = x.astype(BF16)

        def expert(w1_ref, w3_ref, w2_ref, gate):
            h1 = _bdot(xb, w1_ref[0])
            h3 = _bdot(xb, w3_ref[0])
            act = (h1 * _sigmoid(h1)) * h3 * gate
            return _bdot(act.astype(BF16), w2_ref[0])

        y = (expert(w1l_ref, w3l_ref, w2l_ref, route[:, 1:2])
             + expert(w1h_ref, w3h_ref, w2h_ref, route[:, 2:3]))
        lnp = ln_ref[...]
        res = _layer_norm(ALPHA * x + y, lnp[2:3], lnp[3:4])
        row = tile_start + lax.broadcasted_iota(jnp.int32, (MOE_TILE, 1), 0)
        mine = (row >= row_lo) & (row < row_hi)

        @pl.when(row_lo == tile_start)
        def _():
            o_ref[...] = jnp.where(mine, res, 0.0)

        @pl.when(row_lo != tile_start)
        def _():
            o_ref[...] = jnp.where(mine, res, o_ref[...])


def _block_diag_gates(wa, wx):
    def bd(w4):
        return jax.scipy.linalg.block_diag(*[w4[i] for i in range(4)])
    halves = [jnp.concatenate([bd(wa[4 * hf:4 * hf + 4]), bd(wx[4 * hf:4 * hf + 4])], axis=1)
              for hf in range(2)]
    return jnp.stack(halves).astype(BF16)


def kernel(x_prompt, x_sample, state_conv_a, state_conv_b, state_lru_h, ln1_g, ln1_b, w_in, conv_a_w, conv_b_w, conv_b_bias, lru_wa, lru_ba, lru_wx, lru_bx, lru_lambda, w_out, ln2_g, ln2_b, w_router_group, w_router_expert, moe_w1, moe_w3, moe_w2):
    assert w_in.shape[0] == 1, "single-layer trunk"
    B, T, _ = x_prompt.shape
    SB, ST, _ = x_sample.shape
    assert SB == SUBLANES and T % (SUBLANES * PROMPT_CHUNK) == 0
    n_prompt = B * T
    n_sample = SB * ST
    n_tok = n_prompt + n_sample
    assert n_prompt % PERM_BLOCK == 0 and n_sample <= PERM_BLOCK and n_tok % MOE_TILE == 0

    pad_rows = lambda a, rows: jnp.pad(a, ((0, rows - a.shape[0]), (0, 0)))
    weights = (
        w_in[0].astype(BF16),
        _block_diag_gates(lru_wa[0], lru_wx[0]),
        w_out[0].astype(BF16),
        jnp.pad(jnp.concatenate([w_router_group[0], w_router_expert[0]], axis=1),
                ((0, 0), (0, LANES - N_GROUPS - N_EXPERTS))).astype(BF16),
        pad_rows(conv_a_w[0], SUBLANES),
        pad_rows(conv_b_w[0], SUBLANES),
        pad_rows(jnp.stack([conv_b_bias[0], lru_ba[0], lru_bx[0], lru_lambda[0]]), SUBLANES),
        pad_rows(jnp.stack([ln1_g[0], ln1_b[0], ln2_g[0], ln2_b[0]]), SUBLANES),
    )

    C = PROMPT_CHUNK
    nT = T // (SUBLANES * C)
    zeros = lambda rows: jnp.zeros((B, rows, D_HALF), F32)
    x1r_p, cnt_p, uat_p, vbt_p, ht_p = _mixer_call(
        x_prompt.reshape(B * nT * SUBLANES, C, D_MODEL), zeros(2 * SUBLANES), zeros(3 * SUBLANES),
        zeros(SUBLANES), weights, C=C, nB=B, nT=nT, chained=True)

    to_tiles = lambda st: jnp.transpose(st, (1, 0, 2)).reshape(1, -1, D_HALF)
    x1r_s, cnt_s, uat_s, vbt_s, ht_s = _mixer_call(
        x_sample, to_tiles(state_conv_a[0]), to_tiles(state_conv_b[0]), state_lru_h[0][None],
        weights, C=ST, nB=1, nT=1, chained=False)

    rows_p = SUBLANES * C
    route = jnp.concatenate([x1r_p[:, D_MODEL:D_MODEL + 4], x1r_s[:, D_MODEL:D_MODEL + 4]], axis=0)
    bucket = route[:, 0].astype(jnp.int32)
    rank = route[:, 3].astype(jnp.int32)
    counts = jnp.concatenate([cnt_p[:, 0, :N_BUCKETS], cnt_s[:, 0, :N_BUCKETS]], axis=0).astype(jnp.int32)
    totals = jnp.sum(counts, axis=0)
    bucket_start = jnp.cumsum(totals) - totals
    block_start = bucket_start[None, :] + jnp.cumsum(counts, axis=0) - counts
    block_of_row = jnp.minimum(jnp.arange(n_tok, dtype=jnp.int32) // rows_p, counts.shape[0] - 1)
    pos = (jnp.take(block_start.reshape(-1), block_of_row * N_BUCKETS + bucket) + rank).astype(jnp.int32)

    n_pb = n_prompt // PERM_BLOCK
    xs = pl.pallas_call(
        functools.partial(_scatter_kernel, n_prompt_blocks=n_pb, n_sample=n_sample),
        grid_spec=pltpu.PrefetchScalarGridSpec(
            num_scalar_prefetch=1,
            grid=(n_pb + 1,),
            in_specs=[
                pl.BlockSpec((PERM_BLOCK, ROW_W), lambda g, p: (jnp.minimum(g, n_pb - 1), 0)),
                pl.BlockSpec((n_sample, ROW_W), lambda g, p: (0, 0)),
            ],
            out_specs=pl.BlockSpec(memory_space=pl.ANY),
            scratch_shapes=[pltpu.SemaphoreType.DMA((1,))],
        ),
        out_shape=jax.ShapeDtypeStruct((n_tok, ROW_W), F32),
        compiler_params=pltpu.CompilerParams(dimension_semantics=("arbitrary",)),
        name="scatter_rows",
    )(pos, x1r_p, x1r_s)

    n_tiles = n_tok // MOE_TILE
    cuts = jnp.sort(jnp.concatenate([jnp.arange(n_tiles, dtype=jnp.int32) * MOE_TILE,
                                     bucket_start.astype(jnp.int32)]))
    v_lo = cuts
    v_hi = jnp.concatenate([cuts[1:], jnp.array([n_tok], jnp.int32)])
    v_tile = jnp.minimum(v_lo // MOE_TILE, n_tiles - 1)
    v_bucket = jnp.clip(jnp.searchsorted(bucket_start, v_lo, side="right") - 1, 0, N_BUCKETS - 1)
    pair_lo = jnp.array([0, 0, 0, 1, 1, 2], jnp.int32)
    pair_hi = jnp.array([1, 2, 3, 2, 3, 3], jnp.int32)
    v_elo = ((v_bucket // 6) * GROUP_SIZE + pair_lo[v_bucket % 6]).astype(jnp.int32)
    v_ehi = ((v_bucket // 6) * GROUP_SIZE + pair_hi[v_bucket % 6]).astype(jnp.int32)
    n_visits = n_tiles + N_BUCKETS

    w1b = moe_w1[0].astype(BF16)
    w3b = moe_w3[0].astype(BF16)
    w2b = moe_w2[0].astype(BF16)
    up_spec = lambda which: pl.BlockSpec(
        (1, D_MODEL, D_EXPERT), lambda v, vt, vl, vh, el, eh: ((el, eh)[which][v], 0, 0))
    down_spec = lambda which: pl.BlockSpec(
        (1, D_EXPERT, D_MODEL), lambda v, vt, vl, vh, el, eh: ((el, eh)[which][v], 0, 0))
    out_sorted = pl.pallas_call(
        _moe_kernel,
        grid_spec=pltpu.PrefetchScalarGridSpec(
            num_scalar_prefetch=5,
            grid=(n_visits,),
            in_specs=[
                pl.BlockSpec((MOE_TILE, ROW_W), lambda v, vt, vl, vh, el, eh: (vt[v], 0)),
                up_spec(0), up_spec(0), down_spec(0), up_spec(1), up_spec(1), down_spec(1),
                pl.BlockSpec(weights[7].shape, lambda v, vt, vl, vh, el, eh: (0, 0)),
            ],
            out_specs=pl.BlockSpec((MOE_TILE, D_MODEL), lambda v, vt, vl, vh, el, eh: (vt[v], 0)),
        ),
        out_shape=jax.ShapeDtypeStruct((n_tok, D_MODEL), F32),
        compiler_params=pltpu.CompilerParams(
            dimension_semantics=("arbitrary",), vmem_limit_bytes=VMEM_LIMIT),
        name="moe_pairs",
    )(v_tile, v_lo, v_hi, v_elo, v_ehi, xs, w1b, w3b, w2b, w1b, w3b, w2b, weights[7])

    pos_nat = jnp.concatenate([
        pos[:n_prompt].reshape(B * nT, C, SUBLANES).transpose(0, 2, 1).reshape(-1),
        pos[n_prompt:].reshape(ST, SUBLANES).T.reshape(-1)])
    y_p, y_s = pl.pallas_call(
        functools.partial(_gather_kernel, n_prompt_blocks=n_pb, n_sample=n_sample),
        grid_spec=pltpu.PrefetchScalarGridSpec(
            num_scalar_prefetch=1,
            grid=(n_pb + 1,),
            in_specs=[pl.BlockSpec(memory_space=pl.ANY)],
            out_specs=[
                pl.BlockSpec((PERM_BLOCK, D_MODEL), lambda g, p: (jnp.minimum(g, n_pb - 1), 0)),
                pl.BlockSpec((n_sample, D_MODEL), lambda g, p: (0, 0)),
            ],
            scratch_shapes=[pltpu.SemaphoreType.DMA((1,))],
        ),
        out_shape=[jax.ShapeDtypeStruct((n_prompt, D_MODEL), F32),
                   jax.ShapeDtypeStruct((n_sample, D_MODEL), F32)],
        compiler_params=pltpu.CompilerParams(dimension_semantics=("arbitrary",)),
        name="gather_rows",
    )(pos_nat, out_sorted)

    last = SUBLANES - 1
    from_tiles = lambda tl, width: jnp.transpose(tl[0].reshape(width, SUBLANES, D_HALF), (1, 0, 2))[None]
    return (
        y_p.reshape(B, T, D_MODEL),
        y_s.reshape(SB, ST, D_MODEL),
        uat_p.reshape(B, 2, SUBLANES, D_HALF)[:, :, last][None],
        vbt_p.reshape(B, 3, SUBLANES, D_HALF)[:, :, last][None],
        ht_p[:, last][None],
        from_tiles(uat_s, 2),
        from_tiles(vbt_s, 3),
        ht_s,
    )
```

```python
import functools

import jax
import jax.numpy as jnp
from jax import lax
from jax.experimental import pallas as pl
from jax.experimental.pallas import tpu as pltpu

F32 = jnp.float32
BF16 = jnp.bfloat16

D_MODEL = 1024
D_HALF = 512
N_HEADS = 8
HEAD_DIM = 64
LRU_C = 8.0
N_GROUPS = 4
GROUP_SIZE = 4
N_EXPERTS = 16
D_EXPERT = 256
N_BUCKETS = 24
ALPHA = 2.0 ** 0.25
LN_EPS = 1e-5

SUBLANES = 8
LANES = 128
ROW_W = D_MODEL + LANES
VMEM_LIMIT = 56 * 2 ** 20

PROMPT_CHUNK = 64
MOE_TILE = 256
PERM_BLOCK = 512


def _layer_norm(z, g, b):
    mu = jnp.mean(z, axis=-1, keepdims=True)
    zc = z - mu
    var = jnp.mean(zc * zc, axis=-1, keepdims=True)
    return zc * lax.rsqrt(var + LN_EPS) * g + b


def _sigmoid(x):
    return 1.0 / (1.0 + jnp.exp(-x))


def _gelu_tanh(x):
    return 0.5 * x * (1.0 + jnp.tanh(0.7978845608028654 * (x + 0.044715 * (x * x * x))))


def _bdot(a, b):
    return jnp.dot(a, b, preferred_element_type=F32)


def _mixer_kernel(x_hbm, ua0_ref, vb0_ref, h0_ref, win_ref, wg_ref, wout_ref, wr_ref,
                  caw_ref, cbw_ref, vec_ref, ln_ref,
                  x1r_ref, cnt_ref, uat_ref, vbt_ref, ht_ref,
                  xbuf, sem, ubuf, vbuf, a_s, b_s, hl_s, ca_s, uac, vbc, hc, tri,
                  *, C, nT, total, chained):
    R = SUBLANES * C
    t = pl.program_id(1)
    g = pl.program_id(0) * nT + t
    slot = g % 2

    def x_copy(blk, sl, s):
        return pltpu.make_async_copy(x_hbm.at[blk * SUBLANES + s], xbuf.at[sl, :, s, :], sem.at[sl])

    @pl.when(g == 0)
    def _():
        for s in range(SUBLANES):
            x_copy(0, 0, s).start()
        rr = lax.broadcasted_iota(jnp.int32, (R, R), 0)
        cc = lax.broadcasted_iota(jnp.int32, (R, R), 1)
        tri[...] = jnp.where(cc < rr, 1.0, 0.0).astype(BF16)

    for s in range(SUBLANES):
        x_copy(g, slot, s).wait()

    @pl.when(g + 1 < total)
    def _():
        for s in range(SUBLANES):
            x_copy(g + 1, 1 - slot, s).start()

    @pl.when(t == 0)
    def _():
        uac[...] = ua0_ref[0]
        vbc[...] = vb0_ref[0]
        hc[...] = h0_ref[0]

    x = xbuf[slot].reshape(R, D_MODEL)
    xb = x.astype(BF16)

    def proj(k):
        return _bdot(xb, win_ref[:, k * D_HALF:(k + 1) * D_HALF])

    gate_b = proj(0)
    u = proj(1) * proj(2)
    ubuf[2 * SUBLANES:2 * SUBLANES + R, :] = u
    vbuf[3 * SUBLANES:3 * SUBLANES + R, :] = proj(3)
    gate_l = proj(4)

    sub = lax.broadcasted_iota(jnp.int32, (SUBLANES, D_HALF), 0)

    def fill_history(buf, carry, tail_ref, width):
        for k in range(width):
            rows = slice(k * SUBLANES, (k + 1) * SUBLANES)
            tail = buf[R + k * SUBLANES:R + (k + 1) * SUBLANES, :]
            tail_ref[0, rows, :] = tail
            if chained:
                prev_chunk = pltpu.roll(tail, 1, 0)
                buf[rows, :] = jnp.where(sub == 0, carry[rows, :], prev_chunk)
                carry[rows, :] = prev_chunk
            else:
                buf[rows, :] = carry[rows, :]

    fill_history(ubuf, uac, uat_ref, 2)
    fill_history(vbuf, vbc, vbt_ref, 3)

    caw = caw_ref[...]
    uc = (caw[0:1] * ubuf[0:R, :] + caw[1:2] * ubuf[SUBLANES:SUBLANES + R, :]
          + caw[2:3] * ubuf[2 * SUBLANES:2 * SUBLANES + R, :])
    y_a = gate_b * uc

    cbw = cbw_ref[...]
    vec = vec_ref[...]
    xc = (cbw[0:1] * vbuf[0:R, :] + cbw[1:2] * vbuf[SUBLANES:SUBLANES + R, :]
          + cbw[2:3] * vbuf[2 * SUBLANES:2 * SUBLANES + R, :]
          + cbw[3:4] * vbuf[3 * SUBLANES:3 * SUBLANES + R, :]) + vec[0:1]
    xcb = xc.astype(BF16)

    neg_lam = -vec[3:4]
    softplus = jnp.maximum(neg_lam, 0.0) + jnp.log1p(jnp.exp(-jnp.abs(neg_lam)))
    half = D_HALF // 2
    for hf in range(2):
        cs = slice(hf * half, (hf + 1) * half)
        gg = _bdot(xcb[:, cs], wg_ref[hf])
        r = _sigmoid(gg[:, :half] + vec[1:2, cs])
        ig = _sigmoid(gg[:, half:] + vec[2:3, cs])
        log_a = (-LRU_C) * r * softplus[:, cs]
        a = jnp.exp(log_a)
        mult = jnp.sqrt(-jnp.tanh(log_a) * (1.0 + a * a))
        a_s[:, cs] = a
        b_s[:, cs] = mult * (ig * xc[:, cs])

    def scan_step(j, carry):
        hl, ca = carry
        rows = pl.ds(pl.multiple_of(j * SUBLANES, SUBLANES), SUBLANES)
        a = a_s[rows, :]
        hl = a * hl + b_s[rows, :]
        ca = a * ca
        hl_s[rows, :] = hl
        ca_s[rows, :] = ca
        return hl, ca

    zeros = jnp.zeros((SUBLANES, D_HALF), F32)
    b_tot, a_tot = lax.fori_loop(0, C, scan_step, (zeros, zeros + 1.0), unroll=8)

    h_carry = hc[...]
    if chained:
        a_inc, b_inc = a_tot, b_tot
        for d in (1, 2, 4):
            a_sh = pltpu.roll(a_inc, d, 0)
            b_sh = pltpu.roll(b_inc, d, 0)
            keep = sub >= d
            b_inc = jnp.where(keep, a_inc * b_sh + b_inc, b_inc)
            a_inc = jnp.where(keep, a_inc * a_sh, a_inc)
        h_end = b_inc + a_inc * h_carry
        h_in = jnp.where(sub == 0, h_carry, pltpu.roll(h_end, 1, 0))
        hc[...] = jnp.broadcast_to(h_end[SUBLANES - 1:SUBLANES, :], (SUBLANES, D_HALF))
    else:
        h_in = h_carry
        h_end = b_tot + a_tot * h_in
    ht_ref[0] = h_end

    h = (hl_s[...].reshape(C, SUBLANES, D_HALF)
         + ca_s[...].reshape(C, SUBLANES, D_HALF) * h_in[None]).reshape(R, D_HALF)
    y_b = h * _gelu_tanh(gate_l)

    m = _bdot(y_a.astype(BF16), wout_ref[0:D_HALF, :]) + _bdot(y_b.astype(BF16), wout_ref[D_HALF:, :])
    lnp = ln_ref[...]
    x1 = _layer_norm(ALPHA * x + m, lnp[0:1], lnp[1:2])
    x1r_ref[:, 0:D_MODEL] = x1

    logits = _bdot(x1.astype(BF16), wr_ref[...])
    lane = lax.broadcasted_iota(jnp.int32, (R, LANES), 1).astype(F32)
    neg = jnp.float32(-jnp.inf)
    big = jnp.float32(1e9)

    def top1(vals):
        vmax = jnp.max(vals, axis=-1, keepdims=True)
        idx = jnp.min(jnp.where(vals == vmax, lane, big), axis=-1, keepdims=True)
        return vmax, idx

    grp_logits = jnp.where(lane < N_GROUPS, logits, neg)
    gmax, gidx = top1(grp_logits)
    g_gate = 1.0 / jnp.sum(jnp.exp(grp_logits - gmax), axis=-1, keepdims=True)
    e_base = N_GROUPS + GROUP_SIZE * gidx
    e_logits = jnp.where((lane >= e_base) & (lane < e_base + GROUP_SIZE), logits, neg)
    v1, i1 = top1(e_logits)
    v2, i2 = top1(jnp.where(lane == i1, neg, e_logits))
    e21 = jnp.exp(v2 - v1)
    w1 = g_gate / (1.0 + e21)
    w2 = g_gate * e21 / (1.0 + e21)
    l1 = i1 - e_base
    l2 = i2 - e_base
    first_low = l1 < l2
    lo = jnp.where(first_low, l1, l2)
    hi = jnp.where(first_low, l2, l1)
    w_lo = jnp.where(first_low, w1, w2)
    w_hi = jnp.where(first_low, w2, w1)
    pair = lo * (7.0 - lo) * 0.5 + (hi - lo - 1.0)
    bucket = gidx * 6.0 + pair

    onehot = lane == bucket
    before = _bdot(tri[...], jnp.where(onehot, 1.0, 0.0).astype(BF16))
    rank = jnp.sum(jnp.where(onehot, before, 0.0), axis=-1, keepdims=True)
    route = jnp.where(lane == 0, bucket,
                      jnp.where(lane == 1, w_lo,
                                jnp.where(lane == 2, w_hi,
                                          jnp.where(lane == 3, rank, 0.0))))
    x1r_ref[:, D_MODEL:ROW_W] = route
    cnt = jnp.sum(jnp.where(onehot, 1.0, 0.0), axis=0, keepdims=True)
    cnt_ref[0] = jnp.broadcast_to(cnt, (SUBLANES, LANES))


def _mixer_call(x3, ua0, vb0, h0, weights, *, C, nB, nT, chained):
    R = SUBLANES * C
    total = nB * nT
    win, wg, wout, wr, caw, cbw, vec, lnp = weights
    const = lambda shape: pl.BlockSpec(shape, lambda b, t: (0,) * len(shape))
    per_b = lambda rows: pl.BlockSpec((1, rows, D_HALF), lambda b, t: (b, 0, 0))
    kern = functools.partial(_mixer_kernel, C=C, nT=nT, total=total, chained=chained)
    return pl.pallas_call(
        kern,
        grid=(nB, nT),
        in_specs=[
            pl.BlockSpec(memory_space=pl.ANY),
            per_b(2 * SUBLANES), per_b(3 * SUBLANES), per_b(SUBLANES),
            const(win.shape), const(wg.shape), const(wout.shape), const(wr.shape),
            const(caw.shape), const(cbw.shape), const(vec.shape), const(lnp.shape),
        ],
        out_specs=[
            pl.BlockSpec((R, ROW_W), lambda b, t: (b * nT + t, 0)),
            pl.BlockSpec((1, SUBLANES, LANES), lambda b, t: (b * nT + t, 0, 0)),
            per_b(2 * SUBLANES), per_b(3 * SUBLANES), per_b(SUBLANES),
        ],
        out_shape=[
            jax.ShapeDtypeStruct((total * R, ROW_W), F32),
            jax.ShapeDtypeStruct((total, SUBLANES, LANES), F32),
            jax.ShapeDtypeStruct((nB, 2 * SUBLANES, D_HALF), F32),
            jax.ShapeDtypeStruct((nB, 3 * SUBLANES, D_HALF), F32),
            jax.ShapeDtypeStruct((nB, SUBLANES, D_HALF), F32),
        ],
        scratch_shapes=[
            pltpu.VMEM((2, C, SUBLANES, D_MODEL), F32),
            pltpu.SemaphoreType.DMA((2,)),
            pltpu.VMEM((R + 2 * SUBLANES, D_HALF), F32),
            pltpu.VMEM((R + 3 * SUBLANES, D_HALF), F32),
            pltpu.VMEM((R, D_HALF), F32),
            pltpu.VMEM((R, D_HALF), F32),
            pltpu.VMEM((R, D_HALF), F32),
            pltpu.VMEM((R, D_HALF), F32),
            pltpu.VMEM((2 * SUBLANES, D_HALF), F32),
            pltpu.VMEM((3 * SUBLANES, D_HALF), F32),
            pltpu.VMEM((SUBLANES, D_HALF), F32),
            pltpu.VMEM((R, R), BF16),
        ],
        compiler_params=pltpu.CompilerParams(
            dimension_semantics=("arbitrary", "arbitrary"), vmem_limit_bytes=VMEM_LIMIT),
        name="mixer_chained" if chained else "mixer_states",
    )(x3, ua0, vb0, h0, win, wg, wout, wr, caw, cbw, vec, lnp)


def _row_copy(src, dst, sem):
    return pltpu.make_async_copy(src, dst, sem)


def _scatter_kernel(pos_ref, xp_ref, xs_ref, out_hbm, sem, *, n_prompt_blocks, n_sample):
    g = pl.program_id(0)

    def run(src_ref, base, n):
        def start(i, c):
            for k in range(2):
                r = 2 * i + k
                _row_copy(src_ref.at[pl.ds(r, 1)], out_hbm.at[pl.ds(pos_ref[base + r], 1)],
                          sem.at[0]).start(priority=k)
            return c

        lax.fori_loop(0, n // 2, start, 0, unroll=8)
        _row_copy(src_ref, out_hbm.at[pl.ds(0, n)], sem.at[0]).wait()

    @pl.when(g < n_prompt_blocks)
    def _():
        run(xp_ref, g * PERM_BLOCK, PERM_BLOCK)

    @pl.when(g == n_prompt_blocks)
    def _():
        run(xs_ref, n_prompt_blocks * PERM_BLOCK, n_sample)


def _gather_kernel(pos_ref, src_hbm, yp_ref, ys_ref, sem, *, n_prompt_blocks, n_sample):
    g = pl.program_id(0)

    def run(dst_ref, base, n):
        def start(i, c):
            for k in range(2):
                r = 2 * i + k
                _row_copy(src_hbm.at[pl.ds(pos_ref[base + r], 1)], dst_ref.at[pl.ds(r, 1)],
                          sem.at[0]).start(priority=k)
            return c

        lax.fori_loop(0, n // 2, start, 0, unroll=8)
        _row_copy(src_hbm.at[pl.ds(0, n)], dst_ref, sem.at[0]).wait()

    @pl.when(g < n_prompt_blocks)
    def _():
        run(yp_ref, g * PERM_BLOCK, PERM_BLOCK)

    @pl.when(g == n_prompt_blocks)
    def _():
        run(ys_ref, n_prompt_blocks * PERM_BLOCK, n_sample)


def _moe_kernel(vt_ref, vlo_ref, vhi_ref, velo_ref, vehi_ref,
                x_ref, w1l_ref, w3l_ref, w2l_ref, w1h_ref, w3h_ref, w2h_ref, ln_ref, o_ref):
    v = pl.program_id(0)
    row_lo = vlo_ref[v]
    row_hi = vhi_ref[v]
    tile_start = vt_ref[v] * MOE_TILE

    @pl.when(row_hi > row_lo)
    def _():
        x = x_ref[:, 0:D_MODEL]
        route = x_ref[:, D_MODEL:ROW_W]
        xb = x.astype(BF16)

        def expert(w1_ref, w3_ref, w2_ref, gate):
            h1 = _bdot(xb, w1_ref[0])
            h3 = _bdot(xb, w3_ref[0])
            act = (h1 * _sigmoid(h1)) * h3 * gate
            return _bdot(act.astype(BF16), w2_ref[0])

        y = (expert(w1l_ref, w3l_ref, w2l_ref, route[:, 1:2])
             + expert(w1h_ref, w3h_ref, w2h_ref, route[:, 2:3]))
        lnp = ln_ref[...]
        res = _layer_norm(ALPHA * x + y, lnp[2:3], lnp[3:4])
        row = tile_start + lax.broadcasted_iota(jnp.int32, (MOE_TILE, 1), 0)
        mine = (row >= row_lo) & (row < row_hi)

        @pl.when(row_lo == tile_start)
        def _():
            o_ref[...] = jnp.where(mine, res, 0.0)

        @pl.when(row_lo != tile_start)
        def _():
            o_ref[...] = jnp.where(mine, res, o_ref[...])


def _block_diag_gates(wa, wx):
    def bd(w4):
        return jax.scipy.linalg.block_diag(*[w4[i] for i in range(4)])
    halves = [jnp.concatenate([bd(wa[4 * hf:4 * hf + 4]), bd(wx[4 * hf:4 * hf + 4])], axis=1)
              for hf in range(2)]
    return jnp.stack(halves).astype(BF16)


def kernel(x_prompt, x_sample, state_conv_a, state_conv_b, state_lru_h, ln1_g, ln1_b, w_in, conv_a_w, conv_b_w, conv_b_bias, lru_wa, lru_ba, lru_wx, lru_bx, lru_lambda, w_out, ln2_g, ln2_b, w_router_group, w_router_expert, moe_w1, moe_w3, moe_w2):
    assert w_in.shape[0] == 1, "single-layer trunk"
    B, T, _ = x_prompt.shape
    SB, ST, _ = x_sample.shape
    assert SB == SUBLANES and T % (SUBLANES * PROMPT_CHUNK) == 0
    n_prompt = B * T
    n_sample = SB * ST
    n_tok = n_prompt + n_sample
    assert n_prompt % PERM_BLOCK == 0 and n_sample <= PERM_BLOCK and n_tok % MOE_TILE == 0

    pad_rows = lambda a, rows: jnp.pad(a, ((0, rows - a.shape[0]), (0, 0)))
    weights = (
        w_in[0].astype(BF16),
        _block_diag_gates(lru_wa[0], lru_wx[0]),
        w_out[0].astype(BF16),
        jnp.pad(jnp.concatenate([w_router_group[0], w_router_expert[0]], axis=1),
                ((0, 0), (0, LANES - N_GROUPS - N_EXPERTS))).astype(BF16),
        pad_rows(conv_a_w[0], SUBLANES),
        pad_rows(conv_b_w[0], SUBLANES),
        pad_rows(jnp.stack([conv_b_bias[0], lru_ba[0], lru_bx[0], lru_lambda[0]]), SUBLANES),
        pad_rows(jnp.stack([ln1_g[0], ln1_b[0], ln2_g[0], ln2_b[0]]), SUBLANES),
    )

    C = PROMPT_CHUNK
    nT = T // (SUBLANES * C)
    zeros = lambda rows: jnp.zeros((B, rows, D_HALF), F32)
    x1r_p, cnt_p, uat_p, vbt_p, ht_p = _mixer_call(
        x_prompt.reshape(B * nT * SUBLANES, C, D_MODEL), zeros(2 * SUBLANES), zeros(3 * SUBLANES),
        zeros(SUBLANES), weights, C=C, nB=B, nT=nT, chained=True)

    to_tiles = lambda st: jnp.transpose(st, (1, 0, 2)).reshape(1, -1, D_HALF)
    x1r_s, cnt_s, uat_s, vbt_s, ht_s = _mixer_call(
        x_sample, to_tiles(state_conv_a[0]), to_tiles(state_conv_b[0]), state_lru_h[0][None],
        weights, C=ST, nB=1, nT=1, chained=False)

    rows_p = SUBLANES * C
    route = jnp.concatenate([x1r_p[:, D_MODEL:D_MODEL + 4], x1r_s[:, D_MODEL:D_MODEL + 4]], axis=0)
    bucket = route[:, 0].astype(jnp.int32)
    rank = route[:, 3].astype(jnp.int32)
    counts = jnp.concatenate([cnt_p[:, 0, :N_BUCKETS], cnt_s[:, 0, :N_BUCKETS]], axis=0).astype(jnp.int32)
    totals = jnp.sum(counts, axis=0)
    bucket_start = jnp.cumsum(totals) - totals
    block_start = bucket_start[None, :] + jnp.cumsum(counts, axis=0) - counts
    n_blocks_p = n_prompt // rows_p
    start_of_row = jnp.concatenate([
        jnp.broadcast_to(block_start[:n_blocks_p, None, :], (n_blocks_p, rows_p, N_BUCKETS)).reshape(-1, N_BUCKETS),
        jnp.broadcast_to(block_start[n_blocks_p:], (n_sample, N_BUCKETS))], axis=0)
    bucket_ids = jnp.arange(N_BUCKETS, dtype=jnp.int32)
    pos = (jnp.sum(jnp.where(bucket[:, None] == bucket_ids[None, :], start_of_row, 0), axis=1)
           + rank).astype(jnp.int32)

    n_pb = n_prompt // PERM_BLOCK
    xs = pl.pallas_call(
        functools.partial(_scatter_kernel, n_prompt_blocks=n_pb, n_sample=n_sample),
        grid_spec=pltpu.PrefetchScalarGridSpec(
            num_scalar_prefetch=1,
            grid=(n_pb + 1,),
            in_specs=[
                pl.BlockSpec((PERM_BLOCK, ROW_W), lambda g, p: (jnp.minimum(g, n_pb - 1), 0)),
                pl.BlockSpec((n_sample, ROW_W), lambda g, p: (0, 0)),
            ],
            out_specs=pl.BlockSpec(memory_space=pl.ANY),
            scratch_shapes=[pltpu.SemaphoreType.DMA((1,))],
        ),
        out_shape=jax.ShapeDtypeStruct((n_tok, ROW_W), F32),
        compiler_params=pltpu.CompilerParams(dimension_semantics=("arbitrary",)),
        name="scatter_rows",
    )(pos, x1r_p, x1r_s)

    n_tiles = n_tok // MOE_TILE
    n_visits = n_tiles + N_BUCKETS
    cuts = jnp.concatenate([jnp.arange(n_tiles, dtype=jnp.int32) * MOE_TILE, bucket_start.astype(jnp.int32)])
    ids = jnp.arange(n_visits, dtype=jnp.int32)
    before = (cuts[None, :] < cuts[:, None]) | ((cuts[None, :] == cuts[:, None]) & (ids[None, :] < ids[:, None]))
    order = jnp.sum(before.astype(jnp.int32), axis=1)
    v_lo = jnp.sum(jnp.where(order[None, :] == ids[:, None], cuts[None, :], 0), axis=1)
    v_hi = jnp.concatenate([v_lo[1:], jnp.array([n_tok], jnp.int32)])
    v_tile = jnp.minimum(v_lo // MOE_TILE, n_tiles - 1)
    v_bucket = jnp.clip(jnp.sum((bucket_start[None, :] <= v_lo[:, None]).astype(jnp.int32), axis=1) - 1,
                        0, N_BUCKETS - 1)
    pair = v_bucket % 6
    ge3 = (pair >= 3).astype(jnp.int32)
    ge5 = (pair >= 5).astype(jnp.int32)
    v_elo = ((v_bucket // 6) * GROUP_SIZE + ge3 + ge5).astype(jnp.int32)
    v_ehi = ((v_bucket // 6) * GROUP_SIZE + pair + 1 - 2 * ge3 - ge5).astype(jnp.int32)

    w1b = moe_w1[0].astype(BF16)
    w3b = moe_w3[0].astype(BF16)
    w2b = moe_w2[0].astype(BF16)
    up_spec = lambda which: pl.BlockSpec(
        (1, D_MODEL, D_EXPERT), lambda v, vt, vl, vh, el, eh: ((el, eh)[which][v], 0, 0))
    down_spec = lambda which: pl.BlockSpec(
        (1, D_EXPERT, D_MODEL), lambda v, vt, vl, vh, el, eh: ((el, eh)[which][v], 0, 0))
    out_sorted = pl.pallas_call(
        _moe_kernel,
        grid_spec=pltpu.PrefetchScalarGridSpec(
            num_scalar_prefetch=5,
            grid=(n_visits,),
            in_specs=[
                pl.BlockSpec((MOE_TILE, ROW_W), lambda v, vt, vl, vh, el, eh: (vt[v], 0)),
                up_spec(0), up_spec(0), down_spec(0), up_spec(1), up_spec(1), down_spec(1),
                pl.BlockSpec(weights[7].shape, lambda v, vt, vl, vh, el, eh: (0, 0)),
            ],
            out_specs=pl.BlockSpec((MOE_TILE, D_MODEL), lambda v, vt, vl, vh, el, eh: (vt[v], 0)),
        ),
        out_shape=jax.ShapeDtypeStruct((n_tok, D_MODEL), F32),
        compiler_params=pltpu.CompilerParams(
            dimension_semantics=("arbitrary",), vmem_limit_bytes=VMEM_LIMIT),
        name="moe_pairs",
    )(v_tile, v_lo, v_hi, v_elo, v_ehi, xs, w1b, w3b, w2b, w1b, w3b, w2b, weights[7])

    pos_nat = jnp.concatenate([
        pos[:n_prompt].reshape(B * nT, C, SUBLANES).transpose(0, 2, 1).reshape(-1),
        pos[n_prompt:].reshape(ST, SUBLANES).T.reshape(-1)])
    y_p, y_s = pl.pallas_call(
        functools.partial(_gather_kernel, n_prompt_blocks=n_pb, n_sample=n_sample),
        grid_spec=pltpu.PrefetchScalarGridSpec(
            num_scalar_prefetch=1,
            grid=(n_pb + 1,),
            in_specs=[pl.BlockSpec(memory_space=pl.ANY)],
            out_specs=[
                pl.BlockSpec((PERM_BLOCK, D_MODEL), lambda g, p: (jnp.minimum(g, n_pb - 1), 0)),
                pl.BlockSpec((n_sample, D_MODEL), lambda g, p: (0, 0)),
            ],
            scratch_shapes=[pltpu.SemaphoreType.DMA((1,))],
        ),
        out_shape=[jax.ShapeDtypeStruct((n_prompt, D_MODEL), F32),
                   jax.ShapeDtypeStruct((n_sample, D_MODEL), F32)],
        compiler_params=pltpu.CompilerParams(dimension_semantics=("arbitrary",)),
        name="gather_rows",
    )(pos_nat, out_sorted)

    last = SUBLANES - 1
    from_tiles = lambda tl, width: jnp.transpose(tl[0].reshape(width, SUBLANES, D_HALF), (1, 0, 2))[None]
    return (
        y_p.reshape(B, T, D_MODEL),
        y_s.reshape(SB, ST, D_MODEL),
        uat_p.reshape(B, 2, SUBLANES, D_HALF)[:, :, last][None],
        vbt_p.reshape(B, 3, SUBLANES, D_HALF)[:, :, last][None],
        ht_p[:, last][None],
        from_tiles(uat_s, 2),
        from_tiles(vbt_s, 3),
        ht_s,
    )
```

```python
import functools

import numpy as np
import jax
import jax.numpy as jnp
from jax import lax
from jax.experimental import pallas as pl
from jax.experimental.pallas import tpu as pltpu

F32 = jnp.float32
BF16 = jnp.bfloat16

D_MODEL = 1024
D_HALF = 512
N_HEADS = 8
HEAD_DIM = 64
LRU_C = 8.0
N_GROUPS = 4
GROUP_SIZE = 4
N_EXPERTS = 16
D_EXPERT = 256
N_BUCKETS = 24
ALPHA = 2.0 ** 0.25
LN_EPS = 1e-5

SUBLANES = 8
LANES = 128
ROW_W = D_MODEL + LANES
VMEM_LIMIT = 56 * 2 ** 20

PROMPT_CHUNK = 64
MOE_TILE = 256
TILE_SHIFT = 8
SRC_BITS = 16
N_SPARE_TILES = 3


def _layer_norm(z, g, b):
    mu = jnp.mean(z, axis=-1, keepdims=True)
    zc = z - mu
    var = jnp.mean(zc * zc, axis=-1, keepdims=True)
    return zc * lax.rsqrt(var + LN_EPS) * g + b


def _sigmoid(x):
    return 1.0 / (1.0 + jnp.exp(-x))


def _gelu_tanh(x):
    return 0.5 * x * (1.0 + jnp.tanh(0.7978845608028654 * (x + 0.044715 * (x * x * x))))


def _bdot(a, b):
    return jnp.dot(a, b, preferred_element_type=F32)


def _mixer_kernel(x_hbm, rows_prev_hbm, ua0_ref, vb0_ref, h0_ref, win_ref, wg_ref, wout_ref, wr_ref,
                  caw_ref, cbw_ref, vec_ref, ln_ref,
                  x1r_ref, cnt_ref, uat_ref, vbt_ref, ht_ref,
                  xbuf, sem, ubuf, vbuf, a_s, b_s, hl_s, ca_s, uac, vbc, hc, tri,
                  *, C, nT, total, chained):
    R = SUBLANES * C
    t = pl.program_id(1)
    g = pl.program_id(0) * nT + t
    slot = g % 2

    def x_copy(blk, sl, s):
        return pltpu.make_async_copy(x_hbm.at[blk * SUBLANES + s], xbuf.at[sl, :, s, :], sem.at[sl])

    @pl.when(g == 0)
    def _():
        for s in range(SUBLANES):
            x_copy(0, 0, s).start()
        rr = lax.broadcasted_iota(jnp.int32, (R, R), 0)
        cc = lax.broadcasted_iota(jnp.int32, (R, R), 1)
        tri[...] = jnp.where(cc < rr, 1.0, 0.0).astype(BF16)

    for s in range(SUBLANES):
        x_copy(g, slot, s).wait()

    @pl.when(g + 1 < total)
    def _():
        for s in range(SUBLANES):
            x_copy(g + 1, 1 - slot, s).start()

    @pl.when(t == 0)
    def _():
        uac[...] = ua0_ref[0]
        vbc[...] = vb0_ref[0]
        hc[...] = h0_ref[0]

    x = xbuf[slot].reshape(R, D_MODEL)
    xb = x.astype(BF16)

    def proj(k):
        return _bdot(xb, win_ref[:, k * D_HALF:(k + 1) * D_HALF])

    gate_b = proj(0)
    u = proj(1) * proj(2)
    ubuf[2 * SUBLANES:2 * SUBLANES + R, :] = u
    vbuf[3 * SUBLANES:3 * SUBLANES + R, :] = proj(3)
    gate_l = proj(4)

    sub = lax.broadcasted_iota(jnp.int32, (SUBLANES, D_HALF), 0)

    def fill_history(buf, carry, tail_ref, width):
        for k in range(width):
            rows = slice(k * SUBLANES, (k + 1) * SUBLANES)
            tail = buf[R + k * SUBLANES:R + (k + 1) * SUBLANES, :]
            tail_ref[0, rows, :] = tail
            if chained:
                prev_chunk = pltpu.roll(tail, 1, 0)
                buf[rows, :] = jnp.where(sub == 0, carry[rows, :], prev_chunk)
                carry[rows, :] = prev_chunk
            else:
                buf[rows, :] = carry[rows, :]

    fill_history(ubuf, uac, uat_ref, 2)
    fill_history(vbuf, vbc, vbt_ref, 3)

    caw = caw_ref[...]
    uc = (caw[0:1] * ubuf[0:R, :] + caw[1:2] * ubuf[SUBLANES:SUBLANES + R, :]
          + caw[2:3] * ubuf[2 * SUBLANES:2 * SUBLANES + R, :])
    y_a = gate_b * uc

    cbw = cbw_ref[...]
    vec = vec_ref[...]
    xc = (cbw[0:1] * vbuf[0:R, :] + cbw[1:2] * vbuf[SUBLANES:SUBLANES + R, :]
          + cbw[2:3] * vbuf[2 * SUBLANES:2 * SUBLANES + R, :]
          + cbw[3:4] * vbuf[3 * SUBLANES:3 * SUBLANES + R, :]) + vec[0:1]
    xcb = xc.astype(BF16)

    neg_lam = -vec[3:4]
    softplus = jnp.maximum(neg_lam, 0.0) + jnp.log1p(jnp.exp(-jnp.abs(neg_lam)))
    half = D_HALF // 2
    for hf in range(2):
        cs = slice(hf * half, (hf + 1) * half)
        gg = _bdot(xcb[:, cs], wg_ref[hf])
        r = _sigmoid(gg[:, :half] + vec[1:2, cs])
        ig = _sigmoid(gg[:, half:] + vec[2:3, cs])
        log_a = (-LRU_C) * r * softplus[:, cs]
        a = jnp.exp(log_a)
        mult = jnp.sqrt(-jnp.tanh(log_a) * (1.0 + a * a))
        a_s[:, cs] = a
        b_s[:, cs] = mult * (ig * xc[:, cs])

    def scan_step(j, carry):
        hl, ca = carry
        rows = pl.ds(pl.multiple_of(j * SUBLANES, SUBLANES), SUBLANES)
        a = a_s[rows, :]
        hl = a * hl + b_s[rows, :]
        ca = a * ca
        hl_s[rows, :] = hl
        ca_s[rows, :] = ca
        return hl, ca

    zeros = jnp.zeros((SUBLANES, D_HALF), F32)
    b_tot, a_tot = lax.fori_loop(0, C, scan_step, (zeros, zeros + 1.0), unroll=8)

    h_carry = hc[...]
    if chained:
        a_inc, b_inc = a_tot, b_tot
        for d in (1, 2, 4):
            a_sh = pltpu.roll(a_inc, d, 0)
            b_sh = pltpu.roll(b_inc, d, 0)
            keep = sub >= d
            b_inc = jnp.where(keep, a_inc * b_sh + b_inc, b_inc)
            a_inc = jnp.where(keep, a_inc * a_sh, a_inc)
        h_end = b_inc + a_inc * h_carry
        h_in = jnp.where(sub == 0, h_carry, pltpu.roll(h_end, 1, 0))
        hc[...] = jnp.broadcast_to(h_end[SUBLANES - 1:SUBLANES, :], (SUBLANES, D_HALF))
    else:
        h_in = h_carry
        h_end = b_tot + a_tot * h_in
    ht_ref[0] = h_end

    h = (hl_s[...].reshape(C, SUBLANES, D_HALF)
         + ca_s[...].reshape(C, SUBLANES, D_HALF) * h_in[None]).reshape(R, D_HALF)
    y_b = h * _gelu_tanh(gate_l)

    m = _bdot(y_a.astype(BF16), wout_ref[0:D_HALF, :]) + _bdot(y_b.astype(BF16), wout_ref[D_HALF:, :])
    lnp = ln_ref[...]
    x1 = _layer_norm(ALPHA * x + m, lnp[0:1], lnp[1:2])
    x1r_ref[:, 0:D_MODEL] = x1

    logits = _bdot(x1.astype(BF16), wr_ref[...])
    lane = lax.broadcasted_iota(jnp.int32, (R, LANES), 1).astype(F32)
    neg = jnp.float32(-jnp.inf)
    big = jnp.float32(1e9)

    def top1(vals):
        vmax = jnp.max(vals, axis=-1, keepdims=True)
        idx = jnp.min(jnp.where(vals == vmax, lane, big), axis=-1, keepdims=True)
        return vmax, idx

    grp_logits = jnp.where(lane < N_GROUPS, logits, neg)
    gmax, gidx = top1(grp_logits)
    g_gate = 1.0 / jnp.sum(jnp.exp(grp_logits - gmax), axis=-1, keepdims=True)
    e_base = N_GROUPS + GROUP_SIZE * gidx
    e_logits = jnp.where((lane >= e_base) & (lane < e_base + GROUP_SIZE), logits, neg)
    v1, i1 = top1(e_logits)
    v2, i2 = top1(jnp.where(lane == i1, neg, e_logits))
    e21 = jnp.exp(v2 - v1)
    w1 = g_gate / (1.0 + e21)
    w2 = g_gate * e21 / (1.0 + e21)
    l1 = i1 - e_base
    l2 = i2 - e_base
    first_low = l1 < l2
    lo = jnp.where(first_low, l1, l2)
    hi = jnp.where(first_low, l2, l1)
    w_lo = jnp.where(first_low, w1, w2)
    w_hi = jnp.where(first_low, w2, w1)
    pair = lo * (7.0 - lo) * 0.5 + (hi - lo - 1.0)
    bucket = gidx * 6.0 + pair

    onehot = lane == bucket
    before = _bdot(tri[...], jnp.where(onehot, 1.0, 0.0).astype(BF16))
    rank = jnp.sum(jnp.where(onehot, before, 0.0), axis=-1, keepdims=True)
    route = jnp.where(lane == 0, bucket,
                      jnp.where(lane == 1, w_lo,
                                jnp.where(lane == 2, w_hi,
                                          jnp.where(lane == 3, rank, 0.0))))
    x1r_ref[:, D_MODEL:ROW_W] = route
    cnt = jnp.sum(jnp.where(onehot, 1.0, 0.0), axis=0, keepdims=True)
    cnt_ref[0] = jnp.broadcast_to(cnt, (SUBLANES, LANES))


def _mixer_call(x3, rows_prev, ua0, vb0, h0, weights, *, C, nB, nT, chained, n_rows, row_offset):
    R = SUBLANES * C
    total = nB * nT
    assert row_offset % R == 0
    blk0 = row_offset // R
    aliased = rows_prev.shape == (n_rows, ROW_W)
    win, wg, wout, wr, caw, cbw, vec, lnp = weights
    const = lambda shape: pl.BlockSpec(shape, lambda b, t: (0,) * len(shape))
    per_b = lambda rows: pl.BlockSpec((1, rows, D_HALF), lambda b, t: (b, 0, 0))
    kern = functools.partial(_mixer_kernel, C=C, nT=nT, total=total, chained=chained)
    return pl.pallas_call(
        kern,
        grid=(nB, nT),
        in_specs=[
            pl.BlockSpec(memory_space=pl.ANY),
            pl.BlockSpec(memory_space=pl.ANY),
            per_b(2 * SUBLANES), per_b(3 * SUBLANES), per_b(SUBLANES),
            const(win.shape), const(wg.shape), const(wout.shape), const(wr.shape),
            const(caw.shape), const(cbw.shape), const(vec.shape), const(lnp.shape),
        ],
        out_specs=[
            pl.BlockSpec((R, ROW_W), lambda b, t: (blk0 + b * nT + t, 0)),
            pl.BlockSpec((1, SUBLANES, LANES), lambda b, t: (b * nT + t, 0, 0)),
            per_b(2 * SUBLANES), per_b(3 * SUBLANES), per_b(SUBLANES),
        ],
        out_shape=[
            jax.ShapeDtypeStruct((n_rows, ROW_W), F32),
            jax.ShapeDtypeStruct((total, SUBLANES, LANES), F32),
            jax.ShapeDtypeStruct((nB, 2 * SUBLANES, D_HALF), F32),
            jax.ShapeDtypeStruct((nB, 3 * SUBLANES, D_HALF), F32),
            jax.ShapeDtypeStruct((nB, SUBLANES, D_HALF), F32),
        ],
        scratch_shapes=[
            pltpu.VMEM((2, C, SUBLANES, D_MODEL), F32),
            pltpu.SemaphoreType.DMA((2,)),
            pltpu.VMEM((R + 2 * SUBLANES, D_HALF), F32),
            pltpu.VMEM((R + 3 * SUBLANES, D_HALF), F32),
            pltpu.VMEM((R, D_HALF), F32),
            pltpu.VMEM((R, D_HALF), F32),
            pltpu.VMEM((R, D_HALF), F32),
            pltpu.VMEM((R, D_HALF), F32),
            pltpu.VMEM((2 * SUBLANES, D_HALF), F32),
            pltpu.VMEM((3 * SUBLANES, D_HALF), F32),
            pltpu.VMEM((SUBLANES, D_HALF), F32),
            pltpu.VMEM((R, R), BF16),
        ],
        input_output_aliases={1: 0} if aliased else {},
        compiler_params=pltpu.CompilerParams(
            dimension_semantics=("arbitrary", "arbitrary"), vmem_limit_bytes=VMEM_LIMIT),
        name="mixer_chained" if chained else "mixer_states",
    )(x3, rows_prev, ua0, vb0, h0, win, wg, wout, wr, caw, cbw, vec, lnp)


def _moe_kernel(pos_ref, dst_ref, elo_ref, ehi_ref, padlo_ref, padn_ref, npairs_ref,
                x_hbm, w1_hbm, w3_hbm, w2_hbm, ln_ref,
                yp_hbm, ys_hbm,
                wv1, wv3, wv2, xb0, xb1, ob0, ob1, inv, wsem, gsem, ssem,
                *, n_rows, n_prompt):
    src_mask = (1 << SRC_BITS) - 1
    weight_copies = [pltpu.make_async_copy(w1_hbm, wv1, wsem.at[0]),
                     pltpu.make_async_copy(w3_hbm, wv3, wsem.at[1]),
                     pltpu.make_async_copy(w2_hbm, wv2, wsem.at[2])]
    for c in weight_copies:
        c.start()

    def pad_range(b, c):
        lo = padlo_ref[b]

        def one(k, c2):
            slot = lo + k
            spare = n_rows + ((slot >> TILE_SHIFT) & 1) * MOE_TILE + (slot & (MOE_TILE - 1))
            inv[slot] = (slot & (n_prompt - 1)) | (spare << SRC_BITS)
            return c2

        return lax.fori_loop(0, padn_ref[b], one, c)

    lax.fori_loop(0, N_BUCKETS + 1, pad_range, 0)

    def invert(p, c):
        inv[pos_ref[p]] = p | (dst_ref[p] << SRC_BITS)
        return c

    lax.fori_loop(0, n_rows, invert, 0, unroll=8)

    ob0[...] = jnp.zeros_like(ob0)
    ob1[...] = jnp.zeros_like(ob1)
    for c in weight_copies:
        c.wait()

    def gather_start(tile, xbuf, sem):
        base = tile * MOE_TILE
        for r in range(MOE_TILE):
            src = inv[base + r] & src_mask
            pltpu.make_async_copy(x_hbm.at[pl.ds(src, 1)], xbuf.at[pl.ds(r, 1)], sem).start(priority=r % 2)

    def gather_wait(xbuf, sem):
        pltpu.make_async_copy(x_hbm.at[pl.ds(0, MOE_TILE)], xbuf, sem).wait()

    def scatter_start(tile, obuf, sem):
        base = tile * MOE_TILE
        for r in range(MOE_TILE):
            dst = lax.shift_right_logical(inv[base + r], SRC_BITS)

            @pl.when(dst < n_prompt)
            def _():
                pltpu.make_async_copy(obuf.at[pl.ds(r, 1)], yp_hbm.at[pl.ds(dst, 1)], sem).start(priority=r % 2)

            @pl.when(dst >= n_prompt)
            def _():
                pltpu.make_async_copy(obuf.at[pl.ds(r, 1)], ys_hbm.at[pl.ds(dst - n_prompt, 1)],
                                      sem).start(priority=r % 2)

    def scatter_wait(obuf, sem):
        pltpu.make_async_copy(obuf, yp_hbm.at[pl.ds(0, MOE_TILE)], sem).wait()

    lnp = ln_ref[...]

    def compute(tile, xbuf, obuf):
        x = xbuf[:, 0:D_MODEL]
        route = xbuf[:, D_MODEL:ROW_W]
        xb = x.astype(BF16)

        def expert(e, gate):
            h1 = _bdot(xb, wv1[e])
            h3 = _bdot(xb, wv3[e])
            act = (h1 * _sigmoid(h1)) * h3 * gate
            return _bdot(act.astype(BF16), wv2[e])

        y = expert(elo_ref[tile], route[:, 1:2]) + expert(ehi_ref[tile], route[:, 2:3])
        obuf[...] = _layer_norm(ALPHA * x + y, lnp[2:3], lnp[3:4])

    def tile_step(tile, k, mine, other, first_of_pair):
        xbuf, obuf, gs, ss = mine
        xbuf_o, obuf_o, gs_o, ss_o = other
        gather_wait(xbuf, gs)
        if first_of_pair:
            @pl.when(k > 0)
            def _():
                scatter_wait(obuf, ss)
        else:
            scatter_wait(obuf, ss)
        gather_start(tile + 1, xbuf_o, gs_o)
        prev = jnp.where(tile > 0, tile - 1, spare_tile) if first_of_pair else tile - 1
        scatter_start(prev, obuf_o, ss_o)
        compute(tile, xbuf, obuf)

    n_pairs = npairs_ref[0]
    spare_tile = 2 * n_pairs + 1
    slot0 = (xb0, ob0, gsem.at[0], ssem.at[0])
    slot1 = (xb1, ob1, gsem.at[1], ssem.at[1])
    gather_start(0, xb0, gsem.at[0])

    def pair(k, c):
        tile_step(2 * k, k, slot0, slot1, True)
        tile_step(2 * k + 1, k, slot1, slot0, False)
        return c

    lax.fori_loop(0, n_pairs, pair, 0)
    last = 2 * n_pairs - 1
    gather_wait(xb0, gsem.at[0])
    scatter_start(last, ob1, ssem.at[1])
    scatter_wait(ob0, ssem.at[0])
    scatter_wait(ob1, ssem.at[1])


def _block_diag_gates(wa, wx):
    def bd(w4):
        return jax.scipy.linalg.block_diag(*[w4[i] for i in range(4)])
    halves = [jnp.concatenate([bd(wa[4 * hf:4 * hf + 4]), bd(wx[4 * hf:4 * hf + 4])], axis=1)
              for hf in range(2)]
    return jnp.stack(halves).astype(BF16)


def kernel(x_prompt, x_sample, state_conv_a, state_conv_b, state_lru_h, ln1_g, ln1_b, w_in, conv_a_w, conv_b_w, conv_b_bias, lru_wa, lru_ba, lru_wx, lru_bx, lru_lambda, w_out, ln2_g, ln2_b, w_router_group, w_router_expert, moe_w1, moe_w3, moe_w2):
    assert w_in.shape[0] == 1, "single-layer trunk"
    B, T, _ = x_prompt.shape
    SB, ST, _ = x_sample.shape
    C = PROMPT_CHUNK
    rows_p = SUBLANES * C
    assert SB == SUBLANES and T % rows_p == 0
    n_prompt = B * T
    n_sample = SB * ST
    n_tok = n_prompt + n_sample
    assert n_prompt & (n_prompt - 1) == 0 and n_tok % MOE_TILE == 0 and n_tok + 2 * MOE_TILE < 1 << (31 - SRC_BITS)
    assert n_tok <= 1 << SRC_BITS and n_prompt % n_sample == 0

    pad_rows = lambda a, rows: jnp.pad(a, ((0, rows - a.shape[0]), (0, 0)))
    weights = (
        w_in[0].astype(BF16),
        _block_diag_gates(lru_wa[0], lru_wx[0]),
        w_out[0].astype(BF16),
        jnp.pad(jnp.concatenate([w_router_group[0], w_router_expert[0]], axis=1),
                ((0, 0), (0, LANES - N_GROUPS - N_EXPERTS))).astype(BF16),
        pad_rows(conv_a_w[0], SUBLANES),
        pad_rows(conv_b_w[0], SUBLANES),
        pad_rows(jnp.stack([conv_b_bias[0], lru_ba[0], lru_bx[0], lru_lambda[0]]), SUBLANES),
        pad_rows(jnp.stack([ln1_g[0], ln1_b[0], ln2_g[0], ln2_b[0]]), SUBLANES),
    )

    to_tiles = lambda st: jnp.transpose(st, (1, 0, 2)).reshape(1, -1, D_HALF)
    rows_s, cnt_s, uat_s, vbt_s, ht_s = _mixer_call(
        x_sample, jnp.zeros((SUBLANES, LANES), F32),
        to_tiles(state_conv_a[0]), to_tiles(state_conv_b[0]), state_lru_h[0][None],
        weights, C=ST, nB=1, nT=1, chained=False, n_rows=n_tok, row_offset=n_prompt)

    nT = T // rows_p
    zeros = lambda rows: jnp.zeros((B, rows, D_HALF), F32)
    x1r, cnt_p, uat_p, vbt_p, ht_p = _mixer_call(
        x_prompt.reshape(B * nT * SUBLANES, C, D_MODEL), rows_s,
        zeros(2 * SUBLANES), zeros(3 * SUBLANES), zeros(SUBLANES),
        weights, C=C, nB=B, nT=nT, chained=True, n_rows=n_tok, row_offset=0)

    route = x1r[:, D_MODEL:D_MODEL + 4]
    bucket = route[:, 0].astype(jnp.int32)
    rank = route[:, 3].astype(jnp.int32)
    counts = jnp.concatenate([cnt_p[:, 0, :N_BUCKETS], cnt_s[:, 0, :N_BUCKETS]], axis=0).astype(jnp.int32)
    totals = jnp.sum(counts, axis=0)
    padded = ((totals + MOE_TILE - 1) // MOE_TILE) * MOE_TILE
    bucket_end = jnp.cumsum(padded)
    bucket_start = bucket_end - padded
    block_start = bucket_start[None, :] + jnp.cumsum(counts, axis=0) - counts
    n_blocks_p = n_prompt // rows_p
    start_of_row = jnp.concatenate([
        jnp.broadcast_to(block_start[:n_blocks_p, None, :], (n_blocks_p, rows_p, N_BUCKETS)).reshape(-1, N_BUCKETS),
        jnp.broadcast_to(block_start[n_blocks_p:], (n_sample, N_BUCKETS))], axis=0)
    bucket_ids = jnp.arange(N_BUCKETS, dtype=jnp.int32)
    pos = (jnp.sum(jnp.where(bucket[:, None] == bucket_ids[None, :], start_of_row, 0), axis=1)
           + rank).astype(jnp.int32)

    max_tiles = n_tok // MOE_TILE + N_BUCKETS
    n_used = bucket_end[-1] // MOE_TILE
    n_pairs = (n_used + 1) // 2
    tile_lo = jnp.arange(max_tiles + N_SPARE_TILES, dtype=jnp.int32) * MOE_TILE
    tile_bucket = jnp.minimum(jnp.sum((bucket_end[None, :] <= tile_lo[:, None]).astype(jnp.int32), axis=1),
                              N_BUCKETS - 1)
    pair = tile_bucket % 6
    ge3 = (pair >= 3).astype(jnp.int32)
    ge5 = (pair >= 5).astype(jnp.int32)
    tile_elo = ((tile_bucket // 6) * GROUP_SIZE + ge3 + ge5).astype(jnp.int32)
    tile_ehi = ((tile_bucket // 6) * GROUP_SIZE + pair + 1 - 2 * ge3 - ge5).astype(jnp.int32)
    pad_lo = jnp.concatenate([bucket_start + totals, (n_used * MOE_TILE)[None]]).astype(jnp.int32)
    pad_n = jnp.concatenate([padded - totals, jnp.array([N_SPARE_TILES * MOE_TILE], jnp.int32)]).astype(jnp.int32)

    dst_rows = np.concatenate([
        np.arange(n_prompt, dtype=np.int32).reshape(B * nT, SUBLANES, C).transpose(0, 2, 1).reshape(-1),
        n_prompt + np.arange(n_sample, dtype=np.int32).reshape(SUBLANES, ST).T.reshape(-1)])

    n_slots = (max_tiles + N_SPARE_TILES) * MOE_TILE
    w1b = moe_w1[0].astype(BF16)
    w3b = moe_w3[0].astype(BF16)
    w2b = moe_w2[0].astype(BF16)
    any_spec = pl.BlockSpec(memory_space=pl.ANY)
    y_p, y_s = pl.pallas_call(
        functools.partial(_moe_kernel, n_rows=n_tok, n_prompt=n_prompt),
        grid_spec=pltpu.PrefetchScalarGridSpec(
            num_scalar_prefetch=7,
            grid=(1,),
            in_specs=[any_spec, any_spec, any_spec, any_spec,
                      pl.BlockSpec(weights[7].shape, lambda i, *_: (0, 0))],
            out_specs=[any_spec, any_spec],
            scratch_shapes=[
                pltpu.VMEM(w1b.shape, BF16), pltpu.VMEM(w3b.shape, BF16), pltpu.VMEM(w2b.shape, BF16),
                pltpu.VMEM((MOE_TILE, ROW_W), F32), pltpu.VMEM((MOE_TILE, ROW_W), F32),
                pltpu.VMEM((MOE_TILE, D_MODEL), F32), pltpu.VMEM((MOE_TILE, D_MODEL), F32),
                pltpu.SMEM((n_slots,), jnp.int32),
                pltpu.SemaphoreType.DMA((3,)), pltpu.SemaphoreType.DMA((2,)), pltpu.SemaphoreType.DMA((2,)),
            ],
        ),
        out_shape=[jax.ShapeDtypeStruct((n_prompt, D_MODEL), F32),
                   jax.ShapeDtypeStruct((n_sample + 2 * MOE_TILE, D_MODEL), F32)],
        compiler_params=pltpu.CompilerParams(
            dimension_semantics=("arbitrary",), vmem_limit_bytes=VMEM_LIMIT),
        name="moe_sorted",
    )(pos, jnp.asarray(dst_rows), tile_elo, tile_ehi, pad_lo, pad_n, n_pairs[None].astype(jnp.int32),
      x1r, w1b, w3b, w2b, weights[7])

    last = SUBLANES - 1
    from_tiles = lambda tl, width: jnp.transpose(tl[0].reshape(width, SUBLANES, D_HALF), (1, 0, 2))[None]
    return (
        y_p.reshape(B, T, D_MODEL),
        y_s[:n_sample].reshape(SB, ST, D_MODEL),
        uat_p.reshape(B, 2, SUBLANES, D_HALF)[:, :, last][None],
        vbt_p.reshape(B, 3, SUBLANES, D_HALF)[:, :, last][None],
        ht_p[:, last][None],
        from_tiles(uat_s, 2),
        from_tiles(vbt_s, 3),
        ht_s,
    )
```

```python
import functools

import jax
import jax.numpy as jnp
from jax import lax
from jax.experimental import pallas as pl
from jax.experimental.pallas import tpu as pltpu
from jax.experimental.pallas import tpu_sc as plsc

F32 = jnp.float32
BF16 = jnp.bfloat16

D_MODEL = 1024
D_HALF = 512
N_HEADS = 8
HEAD_DIM = 64
LRU_C = 8.0
N_GROUPS = 4
GROUP_SIZE = 4
N_EXPERTS = 16
D_EXPERT = 256
N_BUCKETS = 24
ALPHA = 2.0 ** 0.25
LN_EPS = 1e-5

SUBLANES = 8
LANES = 128
ROW_W = D_MODEL + LANES
VMEM_LIMIT = 56 * 2 ** 20

PROMPT_CHUNK = 64
MOE_TILE = 256
SC_WINDOW = 32
SC_WINDOW_SAMPLE = 8


def _layer_norm(z, g, b):
    mu = jnp.mean(z, axis=-1, keepdims=True)
    zc = z - mu
    var = jnp.mean(zc * zc, axis=-1, keepdims=True)
    return zc * lax.rsqrt(var + LN_EPS) * g + b


def _sigmoid(x):
    return 1.0 / (1.0 + jnp.exp(-x))


def _gelu_tanh(x):
    return 0.5 * x * (1.0 + jnp.tanh(0.7978845608028654 * (x + 0.044715 * (x * x * x))))


def _bdot(a, b):
    return jnp.dot(a, b, preferred_element_type=F32)


def _mixer_kernel(x_hbm, rows_prev_hbm, ua0_ref, vb0_ref, h0_ref, win_ref, wg_ref, wout_ref, wr_ref,
                  caw_ref, cbw_ref, vec_ref, ln_ref,
                  x1r_ref, cnt_ref, uat_ref, vbt_ref, ht_ref,
                  xbuf, sem, ubuf, vbuf, a_s, b_s, hl_s, ca_s, uac, vbc, hc, tri,
                  *, C, nT, total, chained):
    R = SUBLANES * C
    t = pl.program_id(1)
    g = pl.program_id(0) * nT + t
    slot = g % 2

    def x_copy(blk, sl, s):
        return pltpu.make_async_copy(x_hbm.at[blk * SUBLANES + s], xbuf.at[sl, :, s, :], sem.at[sl])

    @pl.when(g == 0)
    def _():
        for s in range(SUBLANES):
            x_copy(0, 0, s).start()
        rr = lax.broadcasted_iota(jnp.int32, (R, R), 0)
        cc = lax.broadcasted_iota(jnp.int32, (R, R), 1)
        tri[...] = jnp.where(cc < rr, 1.0, 0.0).astype(BF16)

    for s in range(SUBLANES):
        x_copy(g, slot, s).wait()

    @pl.when(g + 1 < total)
    def _():
        for s in range(SUBLANES):
            x_copy(g + 1, 1 - slot, s).start()

    @pl.when(t == 0)
    def _():
        uac[...] = ua0_ref[0]
        vbc[...] = vb0_ref[0]
        hc[...] = h0_ref[0]

    x = xbuf[slot].reshape(R, D_MODEL)
    xb = x.astype(BF16)

    def proj(k):
        return _bdot(xb, win_ref[:, k * D_HALF:(k + 1) * D_HALF])

    gate_b = proj(0)
    u = proj(1) * proj(2)
    ubuf[2 * SUBLANES:2 * SUBLANES + R, :] = u
    vbuf[3 * SUBLANES:3 * SUBLANES + R, :] = proj(3)
    gate_l = proj(4)

    sub = lax.broadcasted_iota(jnp.int32, (SUBLANES, D_HALF), 0)

    def fill_history(buf, carry, tail_ref, width):
        for k in range(width):
            rows = slice(k * SUBLANES, (k + 1) * SUBLANES)
            tail = buf[R + k * SUBLANES:R + (k + 1) * SUBLANES, :]
            tail_ref[0, rows, :] = tail
            if chained:
                prev_chunk = pltpu.roll(tail, 1, 0)
                buf[rows, :] = jnp.where(sub == 0, carry[rows, :], prev_chunk)
                carry[rows, :] = prev_chunk
            else:
                buf[rows, :] = carry[rows, :]

    fill_history(ubuf, uac, uat_ref, 2)
    fill_history(vbuf, vbc, vbt_ref, 3)

    caw = caw_ref[...]
    uc = (caw[0:1] * ubuf[0:R, :] + caw[1:2] * ubuf[SUBLANES:SUBLANES + R, :]
          + caw[2:3] * ubuf[2 * SUBLANES:2 * SUBLANES + R, :])
    y_a = gate_b * uc

    cbw = cbw_ref[...]
    vec = vec_ref[...]
    xc = (cbw[0:1] * vbuf[0:R, :] + cbw[1:2] * vbuf[SUBLANES:SUBLANES + R, :]
          + cbw[2:3] * vbuf[2 * SUBLANES:2 * SUBLANES + R, :]
          + cbw[3:4] * vbuf[3 * SUBLANES:3 * SUBLANES + R, :]) + vec[0:1]
    xcb = xc.astype(BF16)

    neg_lam = -vec[3:4]
    softplus = jnp.maximum(neg_lam, 0.0) + jnp.log1p(jnp.exp(-jnp.abs(neg_lam)))
    half = D_HALF // 2
    for hf in range(2):
        cs = slice(hf * half, (hf + 1) * half)
        gg = _bdot(xcb[:, cs], wg_ref[hf])
        r = _sigmoid(gg[:, :half] + vec[1:2, cs])
        ig = _sigmoid(gg[:, half:] + vec[2:3, cs])
        log_a = (-LRU_C) * r * softplus[:, cs]
        a = jnp.exp(log_a)
        mult = jnp.sqrt(-jnp.tanh(log_a) * (1.0 + a * a))
        a_s[:, cs] = a
        b_s[:, cs] = mult * (ig * xc[:, cs])

    def scan_step(j, carry):
        hl, ca = carry
        rows = pl.ds(pl.multiple_of(j * SUBLANES, SUBLANES), SUBLANES)
        a = a_s[rows, :]
        hl = a * hl + b_s[rows, :]
        ca = a * ca
        hl_s[rows, :] = hl
        ca_s[rows, :] = ca
        return hl, ca

    zeros = jnp.zeros((SUBLANES, D_HALF), F32)
    b_tot, a_tot = lax.fori_loop(0, C, scan_step, (zeros, zeros + 1.0), unroll=8)

    h_carry = hc[...]
    if chained:
        a_inc, b_inc = a_tot, b_tot
        for d in (1, 2, 4):
            a_sh = pltpu.roll(a_inc, d, 0)
            b_sh = pltpu.roll(b_inc, d, 0)
            keep = sub >= d
            b_inc = jnp.where(keep, a_inc * b_sh + b_inc, b_inc)
            a_inc = jnp.where(keep, a_inc * a_sh, a_inc)
        h_end = b_inc + a_inc * h_carry
        h_in = jnp.where(sub == 0, h_carry, pltpu.roll(h_end, 1, 0))
        hc[...] = jnp.broadcast_to(h_end[SUBLANES - 1:SUBLANES, :], (SUBLANES, D_HALF))
    else:
        h_in = h_carry
        h_end = b_tot + a_tot * h_in
    ht_ref[0] = h_end

    h = (hl_s[...].reshape(C, SUBLANES, D_HALF)
         + ca_s[...].reshape(C, SUBLANES, D_HALF) * h_in[None]).reshape(R, D_HALF)
    y_b = h * _gelu_tanh(gate_l)

    m = _bdot(y_a.astype(BF16), wout_ref[0:D_HALF, :]) + _bdot(y_b.astype(BF16), wout_ref[D_HALF:, :])
    lnp = ln_ref[...]
    x1 = _layer_norm(ALPHA * x + m, lnp[0:1], lnp[1:2])
    x1r_ref[:, 0:D_MODEL] = x1

    logits = _bdot(x1.astype(BF16), wr_ref[...])
    lane = lax.broadcasted_iota(jnp.int32, (R, LANES), 1).astype(F32)
    neg = jnp.float32(-jnp.inf)
    big = jnp.float32(1e9)

    def top1(vals):
        vmax = jnp.max(vals, axis=-1, keepdims=True)
        idx = jnp.min(jnp.where(vals == vmax, lane, big), axis=-1, keepdims=True)
        return vmax, idx

    grp_logits = jnp.where(lane < N_GROUPS, logits, neg)
    gmax, gidx = top1(grp_logits)
    g_gate = 1.0 / jnp.sum(jnp.exp(grp_logits - gmax), axis=-1, keepdims=True)
    e_base = N_GROUPS + GROUP_SIZE * gidx
    e_logits = jnp.where((lane >= e_base) & (lane < e_base + GROUP_SIZE), logits, neg)
    v1, i1 = top1(e_logits)
    v2, i2 = top1(jnp.where(lane == i1, neg, e_logits))
    e21 = jnp.exp(v2 - v1)
    w1 = g_gate / (1.0 + e21)
    w2 = g_gate * e21 / (1.0 + e21)
    l1 = i1 - e_base
    l2 = i2 - e_base
    first_low = l1 < l2
    lo = jnp.where(first_low, l1, l2)
    hi = jnp.where(first_low, l2, l1)
    w_lo = jnp.where(first_low, w1, w2)
    w_hi = jnp.where(first_low, w2, w1)
    pair = lo * (7.0 - lo) * 0.5 + (hi - lo - 1.0)
    bucket = gidx * 6.0 + pair

    onehot = lane == bucket
    before = _bdot(tri[...], jnp.where(onehot, 1.0, 0.0).astype(BF16))
    rank = jnp.sum(jnp.where(onehot, before, 0.0), axis=-1, keepdims=True)
    route = jnp.where(lane == 0, bucket,
                      jnp.where(lane == 1, w_lo,
                                jnp.where(lane == 2, w_hi,
                                          jnp.where(lane == 3, rank, 0.0))))
    x1r_ref[:, D_MODEL:ROW_W] = route
    cnt = jnp.sum(jnp.where(onehot, 1.0, 0.0), axis=0, keepdims=True)
    cnt_ref[0] = jnp.broadcast_to(cnt, (SUBLANES, LANES))


def _mixer_call(x3, rows_prev, ua0, vb0, h0, weights, *, C, nB, nT, chained, n_rows, row_offset):
    R = SUBLANES * C
    total = nB * nT
    assert row_offset % R == 0
    blk0 = row_offset // R
    aliased = rows_prev.shape == (n_rows, ROW_W)
    win, wg, wout, wr, caw, cbw, vec, lnp = weights
    const = lambda shape: pl.BlockSpec(shape, lambda b, t: (0,) * len(shape))
    per_b = lambda rows: pl.BlockSpec((1, rows, D_HALF), lambda b, t: (b, 0, 0))
    kern = functools.partial(_mixer_kernel, C=C, nT=nT, total=total, chained=chained)
    return pl.pallas_call(
        kern,
        grid=(nB, nT),
        in_specs=[
            pl.BlockSpec(memory_space=pl.ANY),
            pl.BlockSpec(memory_space=pl.ANY),
            per_b(2 * SUBLANES), per_b(3 * SUBLANES), per_b(SUBLANES),
            const(win.shape), const(wg.shape), const(wout.shape), const(wr.shape),
            const(caw.shape), const(cbw.shape), const(vec.shape), const(lnp.shape),
        ],
        out_specs=[
            pl.BlockSpec((R, ROW_W), lambda b, t: (blk0 + b * nT + t, 0)),
            pl.BlockSpec((1, SUBLANES, LANES), lambda b, t: (b * nT + t, 0, 0)),
            per_b(2 * SUBLANES), per_b(3 * SUBLANES), per_b(SUBLANES),
        ],
        out_shape=[
            jax.ShapeDtypeStruct((n_rows, ROW_W), F32),
            jax.ShapeDtypeStruct((total, SUBLANES, LANES), F32),
            jax.ShapeDtypeStruct((nB, 2 * SUBLANES, D_HALF), F32),
            jax.ShapeDtypeStruct((nB, 3 * SUBLANES, D_HALF), F32),
            jax.ShapeDtypeStruct((nB, SUBLANES, D_HALF), F32),
        ],
        scratch_shapes=[
            pltpu.VMEM((2, C, SUBLANES, D_MODEL), F32),
            pltpu.SemaphoreType.DMA((2,)),
            pltpu.VMEM((R + 2 * SUBLANES, D_HALF), F32),
            pltpu.VMEM((R + 3 * SUBLANES, D_HALF), F32),
            pltpu.VMEM((R, D_HALF), F32),
            pltpu.VMEM((R, D_HALF), F32),
            pltpu.VMEM((R, D_HALF), F32),
            pltpu.VMEM((R, D_HALF), F32),
            pltpu.VMEM((2 * SUBLANES, D_HALF), F32),
            pltpu.VMEM((3 * SUBLANES, D_HALF), F32),
            pltpu.VMEM((SUBLANES, D_HALF), F32),
            pltpu.VMEM((R, R), BF16),
        ],
        input_output_aliases={1: 0} if aliased else {},
        compiler_params=pltpu.CompilerParams(
            dimension_semantics=("arbitrary", "arbitrary"), vmem_limit_bytes=VMEM_LIMIT),
        name="mixer_chained" if chained else "mixer_states",
    )(x3, rows_prev, ua0, vb0, h0, win, wg, wout, wr, caw, cbw, vec, lnp)


def _moe_kernel(vt_ref, vlo_ref, vhi_ref, velo_ref, vehi_ref,
                x_ref, w1l_ref, w3l_ref, w2l_ref, w1h_ref, w3h_ref, w2h_ref, ln_ref, o_ref):
    v = pl.program_id(0)
    row_lo = vlo_ref[v]
    row_hi = vhi_ref[v]
    tile_start = vt_ref[v] * MOE_TILE

    @pl.when(row_hi > row_lo)
    def _():
        x = x_ref[:, 0:D_MODEL]
        route = x_ref[:, D_MODEL:ROW_W]
        xb = x.astype(BF16)

        def expert(w1_ref, w3_ref, w2_ref, gate):
            h1 = _bdot(xb, w1_ref[0])
            h3 = _bdot(xb, w3_ref[0])
            act = (h1 * _sigmoid(h1)) * h3 * gate
            return _bdot(act.astype(BF16), w2_ref[0])

        y = (expert(w1l_ref, w3l_ref, w2l_ref, route[:, 1:2])
             + expert(w1h_ref, w3h_ref, w2h_ref, route[:, 2:3]))
        lnp = ln_ref[...]
        res = _layer_norm(ALPHA * x + y, lnp[2:3], lnp[3:4])
        row = tile_start + lax.broadcasted_iota(jnp.int32, (MOE_TILE, 1), 0)
        mine = (row >= row_lo) & (row < row_hi)

        @pl.when(row_lo == tile_start)
        def _():
            o_ref[...] = jnp.where(mine, res, 0.0)

        @pl.when(row_lo != tile_start)
        def _():
            o_ref[...] = jnp.where(mine, res, o_ref[...])


def _sc_mesh():
    return plsc.VectorSubcoreMesh(core_axis_name="core", subcore_axis_name="subcore")


def _sc_share(n_rows, window):
    info = plsc.get_sparse_core_info()
    workers = info.num_cores * info.num_subcores
    assert n_rows % (window * workers) == 0
    return info.num_subcores, n_rows // workers


def _sc_scatter_rows(x, slot_of_row, window):
    n, width = x.shape
    n_sub, share = _sc_share(n, window)

    @functools.partial(pl.kernel, out_type=jax.ShapeDtypeStruct((n, width), x.dtype), mesh=_sc_mesh(),
                       scratch_types=[pltpu.VMEM((share,), jnp.int32), pltpu.VMEM((window, width), x.dtype)])
    def scatter(x_hbm, i_hbm, o_hbm, idx, buf):
        base = (lax.axis_index("core") * n_sub + lax.axis_index("subcore")) * share
        pltpu.sync_copy(i_hbm.at[pl.ds(base, share)], idx)

        @pl.loop(0, share // window)
        def _(t):
            pltpu.sync_copy(x_hbm.at[pl.ds(base + t * window, window)], buf)
            pltpu.sync_copy(buf, o_hbm.at[idx.at[pl.ds(t * window, window)]])

    return scatter(x, slot_of_row)


def _sc_gather_rows(x, row_of_out, window):
    n = row_of_out.shape[0]
    width = x.shape[1]
    n_sub, share = _sc_share(n, window)

    @functools.partial(pl.kernel, out_type=jax.ShapeDtypeStruct((n, width), x.dtype), mesh=_sc_mesh(),
                       scratch_types=[pltpu.VMEM((share,), jnp.int32), pltpu.VMEM((window, width), x.dtype)])
    def gather(x_hbm, i_hbm, o_hbm, idx, buf):
        base = (lax.axis_index("core") * n_sub + lax.axis_index("subcore")) * share
        pltpu.sync_copy(i_hbm.at[pl.ds(base, share)], idx)

        @pl.loop(0, share // window)
        def _(t):
            pltpu.sync_copy(x_hbm.at[idx.at[pl.ds(t * window, window)]], buf)
            pltpu.sync_copy(buf, o_hbm.at[pl.ds(base + t * window, window)])

    return gather(x, row_of_out)


def _block_diag_gates(wa, wx):
    def bd(w4):
        return jax.scipy.linalg.block_diag(*[w4[i] for i in range(4)])
    halves = [jnp.concatenate([bd(wa[4 * hf:4 * hf + 4]), bd(wx[4 * hf:4 * hf + 4])], axis=1)
              for hf in range(2)]
    return jnp.stack(halves).astype(BF16)


def kernel(x_prompt, x_sample, state_conv_a, state_conv_b, state_lru_h, ln1_g, ln1_b, w_in, conv_a_w, conv_b_w, conv_b_bias, lru_wa, lru_ba, lru_wx, lru_bx, lru_lambda, w_out, ln2_g, ln2_b, w_router_group, w_router_expert, moe_w1, moe_w3, moe_w2):
    assert w_in.shape[0] == 1, "single-layer trunk"
    B, T, _ = x_prompt.shape
    SB, ST, _ = x_sample.shape
    C = PROMPT_CHUNK
    rows_p = SUBLANES * C
    assert SB == SUBLANES and T % rows_p == 0
    n_prompt = B * T
    n_sample = SB * ST
    n_tok = n_prompt + n_sample
    assert n_tok % MOE_TILE == 0 and n_prompt % n_sample == 0
    sc = plsc.get_sparse_core_info()
    sc_quantum = SC_WINDOW * sc.num_cores * sc.num_subcores
    n_arr = -(-n_tok // sc_quantum) * sc_quantum

    pad_rows = lambda a, rows: jnp.pad(a, ((0, rows - a.shape[0]), (0, 0)))
    weights = (
        w_in[0].astype(BF16),
        _block_diag_gates(lru_wa[0], lru_wx[0]),
        w_out[0].astype(BF16),
        jnp.pad(jnp.concatenate([w_router_group[0], w_router_expert[0]], axis=1),
                ((0, 0), (0, LANES - N_GROUPS - N_EXPERTS))).astype(BF16),
        pad_rows(conv_a_w[0], SUBLANES),
        pad_rows(conv_b_w[0], SUBLANES),
        pad_rows(jnp.stack([conv_b_bias[0], lru_ba[0], lru_bx[0], lru_lambda[0]]), SUBLANES),
        pad_rows(jnp.stack([ln1_g[0], ln1_b[0], ln2_g[0], ln2_b[0]]), SUBLANES),
    )

    to_tiles = lambda st: jnp.transpose(st, (1, 0, 2)).reshape(1, -1, D_HALF)
    rows_s, cnt_s, uat_s, vbt_s, ht_s = _mixer_call(
        x_sample, jnp.zeros((SUBLANES, LANES), F32),
        to_tiles(state_conv_a[0]), to_tiles(state_conv_b[0]), state_lru_h[0][None],
        weights, C=ST, nB=1, nT=1, chained=False, n_rows=n_arr, row_offset=n_prompt)

    nT = T // rows_p
    zeros = lambda rows: jnp.zeros((B, rows, D_HALF), F32)
    x1r, cnt_p, uat_p, vbt_p, ht_p = _mixer_call(
        x_prompt.reshape(B * nT * SUBLANES, C, D_MODEL), rows_s,
        zeros(2 * SUBLANES), zeros(3 * SUBLANES), zeros(SUBLANES),
        weights, C=C, nB=B, nT=nT, chained=True, n_rows=n_arr, row_offset=0)

    route = x1r[:n_tok, D_MODEL:D_MODEL + 4]
    bucket = route[:, 0].astype(jnp.int32)
    rank = route[:, 3].astype(jnp.int32)
    counts = jnp.concatenate([cnt_p[:, 0, :N_BUCKETS], cnt_s[:, 0, :N_BUCKETS]], axis=0).astype(jnp.int32)
    totals = jnp.sum(counts, axis=0)
    bucket_start = jnp.cumsum(totals) - totals
    block_start = bucket_start[None, :] + jnp.cumsum(counts, axis=0) - counts
    n_blocks_p = n_prompt // rows_p
    start_of_row = jnp.concatenate([
        jnp.broadcast_to(block_start[:n_blocks_p, None, :], (n_blocks_p, rows_p, N_BUCKETS)).reshape(-1, N_BUCKETS),
        jnp.broadcast_to(block_start[n_blocks_p:], (n_sample, N_BUCKETS))], axis=0)
    bucket_ids = jnp.arange(N_BUCKETS, dtype=jnp.int32)
    pos = (jnp.sum(jnp.where(bucket[:, None] == bucket_ids[None, :], start_of_row, 0), axis=1)
           + rank).astype(jnp.int32)

    slot_of_row = jnp.concatenate([pos, jnp.arange(n_tok, n_arr, dtype=jnp.int32)])
    xs = _sc_scatter_rows(x1r, slot_of_row, SC_WINDOW)

    n_tiles = n_tok // MOE_TILE
    n_visits = n_tiles + N_BUCKETS
    cuts = jnp.concatenate([jnp.arange(n_tiles, dtype=jnp.int32) * MOE_TILE, bucket_start.astype(jnp.int32)])
    ids = jnp.arange(n_visits, dtype=jnp.int32)
    before = (cuts[None, :] < cuts[:, None]) | ((cuts[None, :] == cuts[:, None]) & (ids[None, :] < ids[:, None]))
    order = jnp.sum(before.astype(jnp.int32), axis=1)
    v_lo = jnp.sum(jnp.where(order[None, :] == ids[:, None], cuts[None, :], 0), axis=1)
    v_hi = jnp.concatenate([v_lo[1:], jnp.array([n_tok], jnp.int32)])
    v_tile = jnp.minimum(v_lo // MOE_TILE, n_tiles - 1)
    v_bucket = jnp.clip(jnp.sum((bucket_start[None, :] <= v_lo[:, None]).astype(jnp.int32), axis=1) - 1,
                        0, N_BUCKETS - 1)
    pair = v_bucket % 6
    ge3 = (pair >= 3).astype(jnp.int32)
    ge5 = (pair >= 5).astype(jnp.int32)
    v_elo = ((v_bucket // 6) * GROUP_SIZE + ge3 + ge5).astype(jnp.int32)
    v_ehi = ((v_bucket // 6) * GROUP_SIZE + pair + 1 - 2 * ge3 - ge5).astype(jnp.int32)

    w1b = moe_w1[0].astype(BF16)
    w3b = moe_w3[0].astype(BF16)
    w2b = moe_w2[0].astype(BF16)
    up_spec = lambda which: pl.BlockSpec(
        (1, D_MODEL, D_EXPERT), lambda v, vt, vl, vh, el, eh: ((el, eh)[which][v], 0, 0))
    down_spec = lambda which: pl.BlockSpec(
        (1, D_EXPERT, D_MODEL), lambda v, vt, vl, vh, el, eh: ((el, eh)[which][v], 0, 0))
    out_sorted = pl.pallas_call(
        _moe_kernel,
        grid_spec=pltpu.PrefetchScalarGridSpec(
            num_scalar_prefetch=5,
            grid=(n_visits,),
            in_specs=[
                pl.BlockSpec((MOE_TILE, ROW_W), lambda v, vt, vl, vh, el, eh: (vt[v], 0)),
                up_spec(0), up_spec(0), down_spec(0), up_spec(1), up_spec(1), down_spec(1),
                pl.BlockSpec(weights[7].shape, lambda v, vt, vl, vh, el, eh: (0, 0)),
            ],
            out_specs=pl.BlockSpec((MOE_TILE, D_MODEL), lambda v, vt, vl, vh, el, eh: (vt[v], 0)),
        ),
        out_shape=jax.ShapeDtypeStruct((n_tok, D_MODEL), F32),
        compiler_params=pltpu.CompilerParams(
            dimension_semantics=("arbitrary",), vmem_limit_bytes=VMEM_LIMIT),
        name="moe_pairs",
    )(v_tile, v_lo, v_hi, v_elo, v_ehi, xs, w1b, w3b, w2b, w1b, w3b, w2b, weights[7])

    slot_p = pos[:n_prompt].reshape(B * nT, C, SUBLANES).transpose(0, 2, 1).reshape(-1)
    slot_s = pos[n_prompt:].reshape(ST, SUBLANES).T.reshape(-1)
    y_p = _sc_gather_rows(out_sorted, slot_p, SC_WINDOW)
    y_s = _sc_gather_rows(out_sorted, slot_s, SC_WINDOW_SAMPLE)

    last = SUBLANES - 1
    from_tiles = lambda tl, width: jnp.transpose(tl[0].reshape(width, SUBLANES, D_HALF), (1, 0, 2))[None]
    return (
        y_p.reshape(B, T, D_MODEL),
        y_s.reshape(SB, ST, D_MODEL),
        uat_p.reshape(B, 2, SUBLANES, D_HALF)[:, :, last][None],
        vbt_p.reshape(B, 3, SUBLANES, D_HALF)[:, :, last][None],
        ht_p[:, last][None],
        from_tiles(uat_s, 2),
        from_tiles(vbt_s, 3),
        ht_s,
    )
```
